```python
import math
import jax, jax.numpy as jnp
from jax import lax
import numpy as np

D_MODEL = 1024
BATCH = 4
SEQ = 8192
DEPTH = 2

HEAD_DIM = 64
BRANCH_WIDTH = 512
N_BRANCH = 3
CONV_GROUPS = BRANCH_WIDTH // HEAD_DIM
RWKV_HEADS = BRANCH_WIDTH // HEAD_DIM
ATTN_HEADS = BRANCH_WIDTH // HEAD_DIM
CONV_WIDTH = 3
DECAY_RANK = 64
ICLR_RANK = 64
GATE_RANK = 128
D_FF = 2816
Q_BLOCK = 128
NORM_EPS = 1e-6
GN_EPS = 64e-5
DECAY_SCALE = math.exp(-0.5)
FORGET_BIAS_INIT = 2.0

CONV_COLS = 3 * BRANCH_WIDTH
RWKV_COLS = 3 * BRANCH_WIDTH + DECAY_RANK + ICLR_RANK + GATE_RANK
ATTN_COLS = 3 * BRANCH_WIDTH + ATTN_HEADS
GATE_COLS = N_BRANCH * D_MODEL
N_IN = CONV_COLS + RWKV_COLS + ATTN_COLS + GATE_COLS

kernel_name = "hybrid_conv_rwkv7_fox_block"

F32 = jnp.float32


def rms_norm(x, g):
    xf = x.astype(F32)
    y = xf * lax.rsqrt(jnp.mean(xf * xf, axis=-1, keepdims=True) + NORM_EPS)
    return (y * g.astype(F32)).astype(x.dtype)


def causal_dwconv(x, w):
    k_w, c = w.shape
    return lax.conv_general_dilated(
        x, w[:, None, :].astype(x.dtype), window_strides=(1,),
        padding=[(k_w - 1, 0)], dimension_numbers=("NWC", "WIO", "NWC"),
        feature_group_count=c)


def token_shift(z):
    return jnp.concatenate([jnp.zeros_like(z[:, :1]), z[:, :-1]], axis=1)


def conv_mixer(z, w_conv):
    b_gate, c_gate, h = jnp.split(z, 3, axis=-1)
    return b_gate * causal_dwconv(c_gate * h, w_conv)


def rwkv7_mixer(z, mu, w0, w_up, a0, a_up, g_up, k_k, k_a, r_k, gn_g, gn_b):
    bsz, t_len, _ = z.shape
    w_ = BRANCH_WIDTH
    z = z + mu * (token_shift(z) - z)
    r, k, v, wd, ad, gd = jnp.split(
        z, [w_, 2 * w_, 3 * w_, 3 * w_ + DECAY_RANK, 3 * w_ + DECAY_RANK + ICLR_RANK], axis=-1)
    decay = jnp.exp(-DECAY_SCALE * jax.nn.sigmoid((w0 + jnp.tanh(wd) @ w_up).astype(F32)))
    a = jax.nn.sigmoid(a0 + ad @ a_up)
    g = jax.nn.sigmoid(gd) @ g_up
    heads = lambda t: t.astype(F32).reshape(bsz, t_len, RWKV_HEADS, HEAD_DIM)
    kappa = heads(k * k_k)
    kappa_hat = kappa * lax.rsqrt(jnp.sum(kappa * kappa, axis=-1, keepdims=True) + 1e-12)
    k_tilde = heads(k * (1.0 + (a - 1.0) * k_a))
    r_h, v_h, a_h, w_h = heads(r), heads(v), heads(a), heads(decay)

    def step(s, inp):
        r_t, w_t, k_t, v_t, kh_t, a_t = inp
        sk = jnp.einsum("bhvk,bhk->bhv", s, kh_t)
        s = (s * w_t[:, :, None, :] - sk[..., None] * (a_t * kh_t)[:, :, None, :]
             + v_t[..., None] * k_t[:, :, None, :])
        return s, jnp.einsum("bhvk,bhk->bhv", s, r_t)

    xs = tuple(t.transpose(1, 0, 2, 3) for t in (r_h, w_h, k_tilde, v_h, kappa_hat, a_h))
    s0 = jnp.zeros((bsz, RWKV_HEADS, HEAD_DIM, HEAD_DIM), F32)
    _, y = lax.scan(step, s0, xs)
    y = y.transpose(1, 0, 2, 3)
    mean = jnp.mean(y, axis=-1, keepdims=True)
    var = jnp.mean((y - mean) ** 2, axis=-1, keepdims=True)
    y = (y - mean) * lax.rsqrt(var + GN_EPS)
    y = y.reshape(bsz, t_len, w_) * gn_g.astype(F32) + gn_b.astype(F32)
    bonus = jnp.sum(r_h * k_tilde * r_k.astype(F32), axis=-1, keepdims=True) * v_h
    out = (y + bonus.reshape(bsz, t_len, w_)) * g.astype(F32)
    return out.astype(z.dtype)


def forgetting_attention(z, b_f):
    bsz, t_len, _ = z.shape
    w_ = BRANCH_WIDTH
    q, k, v, f_logit = jnp.split(z, [w_, 2 * w_, 3 * w_], axis=-1)
    to_heads = lambda t: t.reshape(bsz, t_len, ATTN_HEADS, HEAD_DIM).transpose(0, 2, 1, 3)
    q, k, v = to_heads(q), to_heads(k), to_heads(v)
    log_f = jax.nn.log_sigmoid((f_logit + b_f).astype(F32))
    c = jnp.cumsum(log_f, axis=1).transpose(0, 2, 1)
    scale = HEAD_DIM ** -0.5
    outs = []
    for i in range(t_len // Q_BLOCK):
        lo, hi = i * Q_BLOCK, (i + 1) * Q_BLOCK
        s = jnp.einsum("bhqd,bhkd->bhqk", q[:, :, lo:hi], k[:, :, :hi]).astype(F32) * scale
        s = s + c[:, :, lo:hi, None] - c[:, :, None, :hi]
        causal = jnp.arange(hi)[None, :] <= jnp.arange(lo, hi)[:, None]
        s = jnp.where(causal, s, -jnp.inf)
        p = jax.nn.softmax(s, axis=-1).astype(v.dtype)
        outs.append(jnp.einsum("bhqk,bhkd->bhqd", p, v[:, :, :hi]))
    o = jnp.concatenate(outs, axis=2)
    return o.transpose(0, 2, 1, 3).reshape(bsz, t_len, w_)


def hybrid_mixer(xn, w_in, gate_b, conv_mix_w, rwkv_mu, rwkv_w0, rwkv_w_up, rwkv_a0,
                 rwkv_a_up, rwkv_g_up, rwkv_k_k, rwkv_k_a, rwkv_r_k, rwkv_gn_g, rwkv_gn_b,
                 attn_forget_b, w_branch, w_o):
    z = xn @ w_in
    z_conv, z_rwkv, z_attn, z_gate = jnp.split(
        z, [CONV_COLS, CONV_COLS + RWKV_COLS, CONV_COLS + RWKV_COLS + ATTN_COLS], axis=-1)
    y_a = conv_mixer(z_conv, conv_mix_w)
    y_b = rwkv7_mixer(z_rwkv, rwkv_mu, rwkv_w0, rwkv_w_up, rwkv_a0, rwkv_a_up, rwkv_g_up,
                      rwkv_k_k, rwkv_k_a, rwkv_r_k, rwkv_gn_g, rwkv_gn_b)
    y_c = forgetting_attention(z_attn, attn_forget_b)
    g_a, g_b, g_c = jnp.split(jax.nn.sigmoid(z_gate + gate_b), 3, axis=-1)
    merged = g_a * (y_a @ w_branch[0]) + g_b * (y_b @ w_branch[1]) + g_c * (y_c @ w_branch[2])
    return merged @ w_o


def conv_ffn(xn, w_up, w_conv, w_down):
    h = causal_dwconv(xn @ w_up, w_conv)
    g, u = jnp.split(h, 2, axis=-1)
    return (jax.nn.silu(g) * u) @ w_down


def setup_inputs(seed: int = 0) -> dict:
    key = jax.random.key(seed)
    ks = jax.random.split(key, 24)
    L, D, W, F = DEPTH, D_MODEL, BRANCH_WIDTH, D_FF
    nrm = lambda k, shape, s: s * jax.random.normal(k, shape, F32)
    return {
        "x": nrm(ks[0], (BATCH, SEQ, D), 1.0),
        "norm1_g": 1.0 + nrm(ks[1], (L, D), 0.05),
        "w_in": nrm(ks[2], (L, D, N_IN), D ** -0.5),
        "gate_b": nrm(ks[3], (L, GATE_COLS), 0.1),
        "conv_mix_w": nrm(ks[4], (L, CONV_WIDTH, W), CONV_WIDTH ** -0.5),
        "rwkv_mu": jax.random.uniform(ks[5], (L, RWKV_COLS), F32),
        "rwkv_w0": nrm(ks[6], (L, W), 0.5),
        "rwkv_w_up": nrm(ks[7], (L, DECAY_RANK, W), DECAY_RANK ** -0.5),
        "rwkv_a0": nrm(ks[8], (L, W), 0.2),
        "rwkv_a_up": nrm(ks[9], (L, ICLR_RANK, W), ICLR_RANK ** -0.5),
        "rwkv_g_up": nrm(ks[10], (L, GATE_RANK, W), GATE_RANK ** -0.5),
        "rwkv_k_k": 0.85 + nrm(ks[11], (L, W), 0.05),
        "rwkv_k_a": 1.0 + nrm(ks[12], (L, W), 0.05),
        "rwkv_r_k": nrm(ks[13], (L, RWKV_HEADS, HEAD_DIM), 0.1),
        "rwkv_gn_g": 1.0 + nrm(ks[14], (L, W), 0.05),
        "rwkv_gn_b": nrm(ks[15], (L, W), 0.02),
        "attn_forget_b": FORGET_BIAS_INIT + nrm(ks[16], (L, ATTN_HEADS), 0.1),
        "w_branch": nrm(ks[17], (L, N_BRANCH, W, D), W ** -0.5),
        "w_o": nrm(ks[18], (L, D, D), D ** -0.5),
        "norm2_g": 1.0 + nrm(ks[19], (L, D), 0.05),
        "ffn_w_up": nrm(ks[20], (L, D, 2 * F), D ** -0.5),
        "ffn_conv_w": nrm(ks[21], (L, CONV_WIDTH, 2 * F), CONV_WIDTH ** -0.5),
        "ffn_w_down": nrm(ks[22], (L, F, D), F ** -0.5),
        "final_norm_g": 1.0 + nrm(ks[23], (D,), 0.05),
    }


def reference(x, norm1_g, w_in, gate_b, conv_mix_w, rwkv_mu, rwkv_w0, rwkv_w_up, rwkv_a0,
              rwkv_a_up, rwkv_g_up, rwkv_k_k, rwkv_k_a, rwkv_r_k, rwkv_gn_g, rwkv_gn_b,
              attn_forget_b, w_branch, w_o, norm2_g, ffn_w_up, ffn_conv_w, ffn_w_down,
              final_norm_g):
    for l in range(DEPTH):
        x = x + hybrid_mixer(
            rms_norm(x, norm1_g[l]), w_in[l], gate_b[l], conv_mix_w[l], rwkv_mu[l],
            rwkv_w0[l], rwkv_w_up[l], rwkv_a0[l], rwkv_a_up[l], rwkv_g_up[l], rwkv_k_k[l],
            rwkv_k_a[l], rwkv_r_k[l], rwkv_gn_g[l], rwkv_gn_b[l], attn_forget_b[l],
            w_branch[l], w_o[l])
        x = x + conv_ffn(rms_norm(x, norm2_g[l]), ffn_w_up[l], ffn_conv_w[l], ffn_w_down[l])
    return rms_norm(x, final_norm_g)
```

```python
import functools
import math

import numpy as np
import jax
import jax.numpy as jnp
from jax import lax
from jax.experimental import pallas as pl
from jax.experimental.pallas import tpu as pltpu

F32 = jnp.float32
BF16 = jnp.bfloat16
HI = lax.Precision.HIGHEST

D_MODEL = 1024
HEAD_DIM = 64
WIDTH = 512
N_PAIR = WIDTH // (2 * HEAD_DIM)
LANES = 128
DECAY_RANK = 64
ICLR_RANK = 64
GATE_RANK = 128
D_FF = 2816
NORM_EPS = 1e-6
GN_EPS = 64e-5
DECAY_SCALE = math.exp(-0.5)
CHUNK = 64

COL_CONV = 0
COL_ATTN = 1536
COL_GATE = 3072
COL_RWKV = 6144
RWKV_BLOCK = 2048
RWKV_USED = 1792
COL_FORGET = COL_RWKV + RWKV_USED
N_PACKED = 8192

VMEM_LIMIT = 56 * 1024 * 1024


def _dot(a, b, precision=None):
    return jnp.dot(a, b, preferred_element_type=F32, precision=precision)


def _dot_nt(a, b, precision=None):
    return lax.dot_general(a, b, (((1,), (1,)), ((), ())),
                           preferred_element_type=F32, precision=precision)


def _split3(x):
    hi = x.astype(BF16)
    r1 = x - hi.astype(F32)
    mid = r1.astype(BF16)
    lo = (r1 - mid.astype(F32)).astype(BF16)
    return hi, mid, lo


def _dot_x01(x, m01):
    hi, mid, lo = _split3(x)
    return _dot(hi, m01) + _dot(mid, m01) + _dot(lo, m01)


def _dot_01x(m01, x):
    hi, mid, lo = _split3(x)
    return _dot(m01, hi) + _dot(m01, mid) + _dot(m01, lo)


def _sigmoid(x):
    return 1.0 / (1.0 + jnp.exp(-x))


def _inproj_kernel(x_ref, g_ref, w_ref, o_ref, xn_ref):
    @pl.when(pl.program_id(1) == 0)
    def _():
        x = x_ref[...]
        ms = jnp.mean(x * x, axis=-1, keepdims=True)
        xn_ref[...] = (x * lax.rsqrt(ms + NORM_EPS) * g_ref[...]).astype(BF16)

    o_ref[...] = _dot(xn_ref[...], w_ref[...])


def _inproj(x2d, g, w_packed, tm=512, tn=1024):
    m = x2d.shape[0]
    return pl.pallas_call(
        _inproj_kernel,
        grid=(m // tm, N_PACKED // tn),
        in_specs=[pl.BlockSpec((tm, D_MODEL), lambda i, j: (i, 0)),
                  pl.BlockSpec((1, D_MODEL), lambda i, j: (0, 0)),
                  pl.BlockSpec((D_MODEL, tn), lambda i, j: (0, j))],
        out_specs=pl.BlockSpec((tm, tn), lambda i, j: (i, j)),
        out_shape=jax.ShapeDtypeStruct((m, N_PACKED), F32),
        scratch_shapes=[pltpu.VMEM((tm, D_MODEL), BF16)],
        compiler_params=pltpu.CompilerParams(
            dimension_semantics=("parallel", "arbitrary"), vmem_limit_bytes=VMEM_LIMIT),
        name="inproj",
    )(x2d, g, w_packed)


def _unit_lower_inverse(a):
    n = a.shape[0]
    r = lax.broadcasted_iota(jnp.int32, (n, n), 0)
    c = lax.broadcasted_iota(jnp.int32, (n, n), 1)
    same16 = (r >> 4) == (c >> 4)
    same32 = (r >> 5) == (c >> 5)
    same64 = (r >> 6) == (c >> 6)
    a16 = jnp.where(same16, a, 0.0)
    t = jnp.where(r == c, 1.0, 0.0) + a16
    p = _dot(a16, a16, HI)
    t = t + _dot(t, p, HI)
    p = _dot(p, p, HI)
    t = t + _dot(t, p, HI)
    p = _dot(p, p, HI)
    t = t + _dot(t, p, HI)
    b32 = jnp.where(same32, jnp.where(same16, 0.0, a), 0.0)
    t = t + _dot(_dot(t, b32, HI), t, HI)
    b64 = jnp.where(same64, jnp.where(same32, 0.0, a), 0.0)
    t = t + _dot(_dot(t, b64, HI), t, HI)
    return t


def _rwkv_kernel(z_ref, mu_ref, wwa_ref, w0a0_ref, gup_ref, vec_ref, seg_ref, tri_ref,
                 y_ref, state_ref, last_ref):
    c = CHUNK

    @pl.when(pl.program_id(1) == 0)
    def _():
        state_ref[...] = jnp.zeros_like(state_ref)
        last_ref[...] = jnp.zeros_like(last_ref)

    z = z_ref[0][:, :RWKV_USED]
    row = lax.broadcasted_iota(jnp.int32, (c, 1), 0)
    prev = jnp.where(row == 0, last_ref[0:1, :], pltpu.roll(z, 1, axis=0))
    last_ref[0:1, :] = z[c - 1:c, :]
    zl = z + mu_ref[...] * (prev - z)

    lane = lax.broadcasted_iota(jnp.int32, (1, LANES), 1)
    first = lane < HEAD_DIM
    wa = zl[:, 3 * WIDTH:3 * WIDTH + LANES]
    wa = jnp.where(first, jnp.tanh(wa), wa)
    wa_out = _dot(wa.astype(BF16), wwa_ref[...]) + w0a0_ref[...]
    gd = zl[:, 3 * WIDTH + LANES:3 * WIDTH + 2 * LANES]
    g_all = _dot(_sigmoid(gd).astype(BF16), gup_ref[...])

    seg = seg_ref[...]
    tri = tri_ref[...]
    r2 = lax.broadcasted_iota(jnp.int32, (2 * c, 2 * c), 0)
    c2 = lax.broadcasted_iota(jnp.int32, (2 * c, 2 * c), 1)
    strict = ((r2 >> 6) == (c2 >> 6)) & ((c2 & (c - 1)) < (r2 & (c - 1)))
    r1 = lax.broadcasted_iota(jnp.int32, (c, 2 * c), 0)
    c1 = lax.broadcasted_iota(jnp.int32, (c, 2 * c), 1)
    incl = (c1 & (c - 1)) <= r1

    def stack(x):
        return jnp.concatenate([jnp.where(first, x, 0.0), jnp.where(first, 0.0, x)], axis=0)

    for p in range(N_PAIR):
        sl = slice(p * LANES, (p + 1) * LANES)
        r_ = zl[:, sl]
        k_ = zl[:, WIDTH + p * LANES:WIDTH + (p + 1) * LANES]
        v_ = zl[:, 2 * WIDTH + p * LANES:2 * WIDTH + (p + 1) * LANES]
        logw = -DECAY_SCALE * _sigmoid(wa_out[:, sl])
        a_ = _sigmoid(wa_out[:, WIDTH + p * LANES:WIDTH + (p + 1) * LANES])
        k_k = vec_ref[0:1, sl]
        k_a = vec_ref[1:2, sl]
        r_k = vec_ref[2:3, sl]
        gn_g = vec_ref[3:4, sl]
        gn_b = vec_ref[4:5, sl]

        kap = k_ * k_k
        kh = kap * lax.rsqrt(_dot_x01(kap * kap, seg) + 1e-12)
        kt = k_ * (1.0 + (a_ - 1.0) * k_a)
        bt = a_ * kh
        bonus = _dot_x01(r_ * kt * r_k, seg) * v_

        cum = _dot_01x(tri, logw)
        cum_x = cum - logw
        cum_c = cum[c - 1:c, :]
        e_in = jnp.exp(cum)
        e_neg = jnp.exp(-cum)
        e_rem = jnp.exp(cum_c - cum)
        qa = -kh * jnp.exp(cum_x)
        qr = r_ * e_in
        qa_st = stack(qa)
        v_st = stack(v_)
        kb2_st = stack(bt * e_rem)
        kk2_st = stack(kt * e_rem)
        q3 = jnp.concatenate([qa_st, qr], axis=0)
        k3 = jnp.concatenate([stack(bt * e_neg), stack(kt * e_neg)], axis=0)
        aa = _dot_nt(q3, k3, HI)
        a_ab = jnp.where(strict, aa[:2 * c, :2 * c], 0.0)
        a_ak = jnp.where(strict, aa[:2 * c, 2 * c:], 0.0)
        a_rb = jnp.where(incl, aa[2 * c:, :2 * c], 0.0)
        a_rk = jnp.where(incl, aa[2 * c:, 2 * c:], 0.0)

        t_inv = _unit_lower_inverse(a_ab)
        tw = _dot(t_inv, jnp.concatenate([_dot(a_ak, v_st, HI), qa_st], axis=1), HI)
        uv_t = tw[:, :LANES].T
        wq_st = tw[:, LANES:]
        vkk = _dot(v_st.T, kk2_st, HI)

        s = state_ref[p]
        u_t = _dot_nt(s, wq_st, HI) + uv_t
        y = _dot_nt(qr, s, HI) + _dot_nt(a_rb, u_t, HI) + _dot(a_rk, v_st, HI)
        state_ref[p] = s * jnp.exp(cum_c) + _dot(u_t, kb2_st, HI) + vkk

        mean = _dot_x01(y, seg) * (1.0 / HEAD_DIM)
        d = y - mean
        var = _dot_x01(d * d, seg) * (1.0 / HEAD_DIM)
        yn = d * lax.rsqrt(var + GN_EPS) * gn_g + gn_b
        y_ref[0, :, sl] = (yn + bonus) * g_all[:, sl]


def _rwkv(z3d, mu, wwa, w0a0, gup, vecs, seg, tri):
    b, t, _ = z3d.shape
    const = lambda shape: pl.BlockSpec(shape, lambda i, j: (0,) * len(shape))
    return pl.pallas_call(
        _rwkv_kernel,
        grid=(b, t // CHUNK),
        in_specs=[pl.BlockSpec((1, CHUNK, RWKV_BLOCK), lambda i, j: (i, j, COL_RWKV // RWKV_BLOCK)),
                  const((1, RWKV_USED)), const((LANES, 2 * WIDTH)), const((1, 2 * WIDTH)),
                  const((GATE_RANK, WIDTH)), const((8, WIDTH)), const((LANES, LANES)),
                  const((CHUNK, CHUNK))],
        out_specs=pl.BlockSpec((1, CHUNK, WIDTH), lambda i, j: (i, j, 0)),
        out_shape=jax.ShapeDtypeStruct((b, t, WIDTH), F32),
        scratch_shapes=[pltpu.VMEM((N_PAIR, LANES, LANES), F32),
                        pltpu.VMEM((8, RWKV_USED), F32)],
        compiler_params=pltpu.CompilerParams(
            dimension_semantics=("parallel", "arbitrary"), vmem_limit_bytes=VMEM_LIMIT),
        name="rwkv7",
    )(z3d, mu, wwa, w0a0, gup, vecs, seg, tri)


def _forget_kernel(z_ref, b_ref, tri_ref, selq_ref, selk_ref, oneq_ref, onek_ref,
                   aq_ref, ak_ref, carry_ref):
    @pl.when(pl.program_id(1) == 0)
    def _():
        carry_ref[...] = jnp.zeros_like(carry_ref)

    f = z_ref[0] + b_ref[...]
    lf = jnp.minimum(f, 0.0) - jnp.log(1.0 + jnp.exp(-jnp.abs(f)))
    cum = _dot_01x(tri_ref[...], lf) + carry_ref[0:1, :]
    n = cum.shape[0]
    carry_ref[0:1, :] = cum[n - 1:n, :]
    parts = jnp.concatenate(_split3(cum), axis=1)
    aq = _dot(parts, selq_ref[...]) + oneq_ref[...]
    ak = _dot(parts, selk_ref[...]) + onek_ref[...]
    for p in range(N_PAIR):
        for e in range(2):
            i = 2 * p + e
            aq_ref[0, p, e] = aq[:, i * LANES:(i + 1) * LANES].astype(BF16)
        ak_ref[0, p] = ak[:, p * LANES:(p + 1) * LANES].astype(BF16)


def _forget(z3d, b_f, tri, selq, selk, oneq, onek, tf=256):
    b, t, _ = z3d.shape
    const = lambda shape: pl.BlockSpec(shape, lambda i, j: (0,) * len(shape))
    return pl.pallas_call(
        _forget_kernel,
        grid=(b, t // tf),
        in_specs=[pl.BlockSpec((1, tf, LANES), lambda i, j: (i, j, COL_FORGET // LANES)),
                  const((1, LANES)), const((tf, tf)), const((3 * LANES, 8 * LANES)),
                  const((3 * LANES, 4 * LANES)), const((1, 8 * LANES)), const((1, 4 * LANES))],
        out_specs=[pl.BlockSpec((1, N_PAIR, 2, tf, LANES), lambda i, j: (i, 0, 0, j, 0)),
                   pl.BlockSpec((1, N_PAIR, tf, LANES), lambda i, j: (i, 0, j, 0))],
        out_shape=[jax.ShapeDtypeStruct((b, N_PAIR, 2, t, LANES), BF16),
                   jax.ShapeDtypeStruct((b, N_PAIR, t, LANES), BF16)],
        scratch_shapes=[pltpu.VMEM((8, LANES), F32)],
        compiler_params=pltpu.CompilerParams(
            dimension_semantics=("parallel", "arbitrary"), vmem_limit_bytes=VMEM_LIMIT),
        name="forget_cumsum",
    )(z3d, b_f, tri, selq, selk, oneq, onek)


def _forget_constants(tf):
    selq = np.zeros((3 * LANES, 8 * LANES), np.float32)
    selk = np.zeros((3 * LANES, N_PAIR * LANES), np.float32)
    oneq = np.zeros((1, 8 * LANES), np.float32)
    onek = np.zeros((1, N_PAIR * LANES), np.float32)
    for p in range(N_PAIR):
        for e in range(2):
            h = 2 * p + e
            for part in range(3):
                selq[part * LANES + h, h * LANES + 8 * e + part] = 1.0
                oneq[0, h * LANES + 8 * e + 3 + part] = 1.0
                selk[part * LANES + h, p * LANES + 8 * e + 3 + part] = -1.0
                onek[0, p * LANES + 8 * e + part] = 1.0
    tri = np.tril(np.ones((tf, tf), np.float32))
    return (jnp.asarray(tri, BF16), jnp.asarray(selq, BF16), jnp.asarray(selk, BF16),
            jnp.asarray(oneq), jnp.asarray(onek))


def _attn_kernel(q_ref, k_ref, v_ref, aq_ref, ak_ref, o_ref, qst_ref, m_ref, l_ref, acc_ref,
                 *, tq, tk):
    i = pl.program_id(2)
    j = pl.program_id(3)
    last_j = (i * tq + tq - 1) // tk
    lane = lax.broadcasted_iota(jnp.int32, (1, LANES), 1)
    first = lane < HEAD_DIM

    @pl.when(j == 0)
    def _():
        q = q_ref[0] * (HEAD_DIM ** -0.5)
        q0 = jnp.concatenate([jnp.where(first, q, 0.0).astype(BF16), aq_ref[0, 0, 0]], axis=1)
        q1 = jnp.concatenate([jnp.where(first, 0.0, q).astype(BF16), aq_ref[0, 0, 1]], axis=1)
        qst_ref[...] = jnp.concatenate([q0, q1], axis=0)
        m_ref[...] = jnp.full_like(m_ref, -1e30)
        l_ref[...] = jnp.zeros_like(l_ref)
        acc_ref[...] = jnp.zeros_like(acc_ref)

    @pl.when(j <= last_j)
    def _():
        k_aug = jnp.concatenate([k_ref[0].astype(BF16), ak_ref[0, 0]], axis=1)
        s = _dot_nt(qst_ref[...], k_aug)
        row = (lax.broadcasted_iota(jnp.int32, (2 * tq, tk), 0) & (tq - 1)) + i * tq
        col = lax.broadcasted_iota(jnp.int32, (2 * tq, tk), 1) + j * tk
        s = jnp.where(col <= row, s, -1e30)
        m_prev = m_ref[...]
        m_new = jnp.maximum(m_prev, jnp.max(s, axis=1, keepdims=True))
        alpha = jnp.exp(m_prev - m_new)
        p = jnp.exp(s - m_new)
        l_ref[...] = alpha * l_ref[...] + jnp.sum(p, axis=1, keepdims=True)
        acc_ref[...] = alpha * acc_ref[...] + _dot(p.astype(BF16), v_ref[0].astype(BF16))
        m_ref[...] = m_new

    @pl.when(j == last_j)
    def _():
        o = acc_ref[...] / l_ref[...]
        o_ref[0] = jnp.where(first, o[:tq], o[tq:])


def _attention(z3d, aq, ak, tq=256, tk=256):
    b, t, _ = z3d.shape
    qb, kb, vb = (COL_ATTN // LANES, (COL_ATTN + WIDTH) // LANES, (COL_ATTN + 2 * WIDTH) // LANES)
    kv_j = lambda i, j: jnp.minimum(j, (i * tq + tq - 1) // tk)
    return pl.pallas_call(
        functools.partial(_attn_kernel, tq=tq, tk=tk),
        grid=(b, N_PAIR, t // tq, t // tk),
        in_specs=[pl.BlockSpec((1, tq, LANES), lambda bi, p, i, j: (bi, i, qb + p)),
                  pl.BlockSpec((1, tk, LANES), lambda bi, p, i, j: (bi, kv_j(i, j), kb + p)),
                  pl.BlockSpec((1, tk, LANES), lambda bi, p, i, j: (bi, kv_j(i, j), vb + p)),
                  pl.BlockSpec((1, 1, 2, tq, LANES), lambda bi, p, i, j: (bi, p, 0, i, 0)),
                  pl.BlockSpec((1, 1, tk, LANES), lambda bi, p, i, j: (bi, p, kv_j(i, j), 0))],
        out_specs=pl.BlockSpec((1, tq, LANES), lambda bi, p, i, j: (bi, i, p)),
        out_shape=jax.ShapeDtypeStruct((b, t, WIDTH), F32),
        scratch_shapes=[pltpu.VMEM((2 * tq, 2 * LANES), BF16),
                        pltpu.VMEM((2 * tq, 1), F32),
                        pltpu.VMEM((2 * tq, 1), F32),
                        pltpu.VMEM((2 * tq, LANES), F32)],
        compiler_params=pltpu.CompilerParams(
            dimension_semantics=("parallel", "parallel", "parallel", "arbitrary"),
            vmem_limit_bytes=VMEM_LIMIT),
        name="forget_attention",
    )(z3d, z3d, z3d, aq, ak)


def _shift_rows(u, carry_ref, tm):
    row = lax.broadcasted_iota(jnp.int32, (tm, 1), 0)
    prev1 = carry_ref[1:2, :]
    prev2 = carry_ref[0:1, :]
    u1 = jnp.where(row == 0, prev1, pltpu.roll(u, 1, axis=0))
    u2 = jnp.where(row == 0, prev2, jnp.where(row == 1, prev1, pltpu.roll(u, 2, axis=0)))
    carry_ref[0:2, :] = u[tm - 2:tm, :]
    return u1, u2


def _merge_kernel(zc_ref, zg_ref, yb_ref, yc_ref, x_ref, cw_ref, gb_ref, wb_ref, wo_ref,
                  o_ref, carry_ref, *, tm):
    @pl.when(pl.program_id(1) == 0)
    def _():
        carry_ref[...] = jnp.zeros_like(carry_ref)

    zc = zc_ref[0]
    u = zc[:, WIDTH:2 * WIDTH] * zc[:, 2 * WIDTH:]
    u1, u2 = _shift_rows(u, carry_ref, tm)
    ya = zc[:, :WIDTH] * (cw_ref[0:1, :] * u2 + cw_ref[1:2, :] * u1 + cw_ref[2:3, :] * u)
    gates = _sigmoid(zg_ref[0] + gb_ref[...])
    merged = (gates[:, :D_MODEL] * _dot(ya.astype(BF16), wb_ref[0])
              + gates[:, D_MODEL:2 * D_MODEL] * _dot(yb_ref[0].astype(BF16), wb_ref[1])
              + gates[:, 2 * D_MODEL:] * _dot(yc_ref[0].astype(BF16), wb_ref[2]))
    o_ref[0] = x_ref[0] + _dot(merged.astype(BF16), wo_ref[...])


def _merge(z3d, yb, yc, x3d, conv_w, gate_b, w_branch, w_o, tm=256):
    b, t, _ = z3d.shape
    const = lambda shape: pl.BlockSpec(shape, lambda i, j: (0,) * len(shape))
    tile = lambda w, col: pl.BlockSpec((1, tm, w), lambda i, j: (i, j, col))
    return pl.pallas_call(
        functools.partial(_merge_kernel, tm=tm),
        grid=(b, t // tm),
        in_specs=[tile(3 * WIDTH, COL_CONV // (3 * WIDTH)), tile(3 * D_MODEL, COL_GATE // (3 * D_MODEL)),
                  tile(WIDTH, 0), tile(WIDTH, 0), tile(D_MODEL, 0),
                  const((8, WIDTH)), const((1, 3 * D_MODEL)),
                  const((3, WIDTH, D_MODEL)), const((D_MODEL, D_MODEL))],
        out_specs=tile(D_MODEL, 0),
        out_shape=jax.ShapeDtypeStruct((b, t, D_MODEL), F32),
        scratch_shapes=[pltpu.VMEM((8, WIDTH), F32)],
        compiler_params=pltpu.CompilerParams(
            dimension_semantics=("parallel", "arbitrary"), vmem_limit_bytes=VMEM_LIMIT),
        name="merge",
    )(z3d, z3d, yb, yc, x3d, conv_w, gate_b, w_branch, w_o)


def _ffn_kernel(x_ref, g_ref, wg_ref, wu_ref, cg_ref, cu_ref, wd_ref, fg_ref, o_ref,
                xn_ref, acc_ref, carry_g_ref, carry_u_ref, *, tm, final):
    f = pl.program_id(2)
    nf = pl.num_programs(2)

    @pl.when((pl.program_id(1) == 0) & (f == 0))
    def _():
        carry_g_ref[...] = jnp.zeros_like(carry_g_ref)
        carry_u_ref[...] = jnp.zeros_like(carry_u_ref)

    @pl.when(f == 0)
    def _():
        x = x_ref[0]
        ms = jnp.mean(x * x, axis=-1, keepdims=True)
        xn_ref[...] = (x * lax.rsqrt(ms + NORM_EPS) * g_ref[...]).astype(BF16)
        acc_ref[...] = jnp.zeros_like(acc_ref)

    xn = xn_ref[...]

    def conv(h, carry_ref, cw_ref):
        h1, h2 = _shift_rows(h, carry_ref.at[f], tm)
        return cw_ref[0:1, :] * h2 + cw_ref[1:2, :] * h1 + cw_ref[2:3, :] * h

    hg = conv(_dot(xn, wg_ref[...]), carry_g_ref, cg_ref)
    hu = conv(_dot(xn, wu_ref[...]), carry_u_ref, cu_ref)
    act = hg * _sigmoid(hg) * hu
    acc_ref[...] += _dot(act.astype(BF16), wd_ref[...])

    @pl.when(f == nf - 1)
    def _():
        y = x_ref[0] + acc_ref[...]
        if final:
            ms = jnp.mean(y * y, axis=-1, keepdims=True)
            y = y * lax.rsqrt(ms + NORM_EPS) * fg_ref[...]
        o_ref[0] = y


def _ffn(x3d, g, w_up, conv_w, w_down, final_g, final, tm=512, fc=1408):
    b, t, _ = x3d.shape
    nf = D_FF // fc
    return pl.pallas_call(
        functools.partial(_ffn_kernel, tm=tm, final=final),
        grid=(b, t // tm, nf),
        in_specs=[pl.BlockSpec((1, tm, D_MODEL), lambda i, j, f: (i, j, 0)),
                  pl.BlockSpec((1, D_MODEL), lambda i, j, f: (0, 0)),
                  pl.BlockSpec((D_MODEL, fc), lambda i, j, f: (0, f)),
                  pl.BlockSpec((D_MODEL, fc), lambda i, j, f: (0, f + nf)),
                  pl.BlockSpec((8, fc), lambda i, j, f: (0, f)),
                  pl.BlockSpec((8, fc), lambda i, j, f: (0, f + nf)),
                  pl.BlockSpec((fc, D_MODEL), lambda i, j, f: (f, 0)),
                  pl.BlockSpec((1, D_MODEL), lambda i, j, f: (0, 0))],
        out_specs=pl.BlockSpec((1, tm, D_MODEL), lambda i, j, f: (i, j, 0)),
        out_shape=jax.ShapeDtypeStruct((b, t, D_MODEL), F32),
        scratch_shapes=[pltpu.VMEM((tm, D_MODEL), BF16),
                        pltpu.VMEM((tm, D_MODEL), F32),
                        pltpu.VMEM((nf, 8, fc), F32),
                        pltpu.VMEM((nf, 8, fc), F32)],
        compiler_params=pltpu.CompilerParams(
            dimension_semantics=("parallel", "arbitrary", "arbitrary"),
            vmem_limit_bytes=VMEM_LIMIT),
        name="conv_ffn",
    )(x3d, g, w_up, w_up, conv_w, conv_w, w_down, final_g)


def _pad_rows(a, rows=8):
    return jnp.pad(a, ((0, rows - a.shape[0]), (0, 0)))


def _pack_w_in(w):
    conv_cols = 3 * WIDTH
    rwkv_cols = 3 * WIDTH + DECAY_RANK + ICLR_RANK + GATE_RANK
    attn_cols = 3 * WIDTH
    n_heads = WIDTH // HEAD_DIM
    o1 = conv_cols
    o2 = o1 + rwkv_cols
    o3 = o2 + attn_cols
    o4 = o3 + n_heads
    d = w.shape[0]
    pad_f = jnp.zeros((d, LANES - n_heads), w.dtype)
    pad_tail = jnp.zeros((d, N_PACKED - COL_FORGET - LANES), w.dtype)
    packed = jnp.concatenate(
        [w[:, :o1], w[:, o2:o3], w[:, o4:], w[:, o1:o2], w[:, o3:o4], pad_f, pad_tail], axis=1)
    return packed.astype(BF16)


def kernel(x, norm1_g, w_in, gate_b, conv_mix_w, rwkv_mu, rwkv_w0, rwkv_w_up, rwkv_a0,
           rwkv_a_up, rwkv_g_up, rwkv_k_k, rwkv_k_a, rwkv_r_k, rwkv_gn_g, rwkv_gn_b,
           attn_forget_b, w_branch, w_o, norm2_g, ffn_w_up, ffn_conv_w, ffn_w_down,
           final_norm_g):
    b, t, d = x.shape
    depth = w_in.shape[0]
    n_heads = WIDTH // HEAD_DIM
    tf = 256

    lane = np.arange(LANES)
    seg = jnp.asarray((lane[:, None] // HEAD_DIM) == (lane[None, :] // HEAD_DIM), BF16)
    tri_c = jnp.asarray(np.tril(np.ones((CHUNK, CHUNK), np.float32)), BF16)
    tri_f, selq, selk, oneq, onek = _forget_constants(tf)

    for l in range(depth):
        w_packed = _pack_w_in(w_in[l])
        z = _inproj(x.reshape(b * t, d), norm1_g[l][None, :], w_packed).reshape(b, t, N_PACKED)

        wwa = jnp.zeros((LANES, 2 * WIDTH), F32)
        wwa = wwa.at[:DECAY_RANK, :WIDTH].set(rwkv_w_up[l]).at[DECAY_RANK:, WIDTH:].set(rwkv_a_up[l])
        w0a0 = jnp.concatenate([rwkv_w0[l], rwkv_a0[l]])[None, :]
        vecs = _pad_rows(jnp.stack([rwkv_k_k[l], rwkv_k_a[l], rwkv_r_k[l].reshape(WIDTH),
                                    rwkv_gn_g[l], rwkv_gn_b[l]]))
        y_b = _rwkv(z, rwkv_mu[l][None, :], wwa.astype(BF16), w0a0,
                    rwkv_g_up[l].astype(BF16), vecs, seg, tri_c)

        b_f = jnp.pad(attn_forget_b[l], (0, LANES - n_heads))[None, :]
        aq, ak = _forget(z, b_f, tri_f, selq, selk, oneq, onek, tf=tf)
        y_c = _attention(z, aq, ak)

        x = _merge(z, y_b, y_c, x, _pad_rows(conv_mix_w[l]), gate_b[l][None, :],
                   w_branch[l].astype(BF16), w_o[l].astype(BF16))
        x = _ffn(x, norm2_g[l][None, :], ffn_w_up[l].astype(BF16), _pad_rows(ffn_conv_w[l]),
                 ffn_w_down[l].astype(BF16), final_norm_g[None, :], final=(l == depth - 1))
    return x
```

```python
import functools
import math

import numpy as np
import jax
import jax.numpy as jnp
from jax import lax
from jax.experimental import pallas as pl
from jax.experimental.pallas import tpu as pltpu

F32 = jnp.float32
BF16 = jnp.bfloat16

D_MODEL = 1024
HEAD_DIM = 64
WIDTH = 512
N_PAIR = WIDTH // (2 * HEAD_DIM)
LANES = 128
DECAY_RANK = 64
ICLR_RANK = 64
GATE_RANK = 128
D_FF = 2816
NORM_EPS = 1e-6
GN_EPS = 64e-5
DECAY_SCALE = math.exp(-0.5)
CHUNK = 64
LOG2E = math.log2(math.e)
NEG_BIG = -1e30

GC_COLS = 3 * D_MODEL + 3 * WIDTH
RW_COLS = 2048
RW_USED = 1792
AT_COLS = 3 * WIDTH

VMEM_LIMIT = 56 * 1024 * 1024


def _dot(a, b):
    return jnp.dot(a, b, preferred_element_type=F32)


def _dot_nt(a, b):
    return lax.dot_general(a, b, (((1,), (1,)), ((), ())), preferred_element_type=F32)


def _bdot(a, b):
    return _dot(a.astype(BF16), b.astype(BF16))


def _bdot_nt(a, b):
    return _dot_nt(a.astype(BF16), b.astype(BF16))


def _split3(x):
    hi = x.astype(BF16)
    r1 = x - hi.astype(F32)
    mid = r1.astype(BF16)
    lo = (r1 - mid.astype(F32)).astype(BF16)
    return hi, mid, lo


def _dot_x01(x, m01):
    hi, mid, lo = _split3(x)
    return _dot(hi, m01) + _dot(mid, m01) + _dot(lo, m01)


def _dot_01x(m01, x):
    hi, mid, lo = _split3(x)
    return _dot(m01, hi) + _dot(m01, mid) + _dot(m01, lo)


def _sigmoid(x):
    return 1.0 / (1.0 + jnp.exp(-x))


def _inproj_kernel(x_ref, g_ref, w_ref, o_ref, xn_ref):
    @pl.when(pl.program_id(1) == 0)
    def _():
        x = x_ref[...]
        ms = jnp.mean(x * x, axis=-1, keepdims=True)
        xn_ref[...] = (x * lax.rsqrt(ms + NORM_EPS) * g_ref[...]).astype(BF16)

    o_ref[...] = _dot(xn_ref[...], w_ref[...])


def _inproj(x2d, g, w, tn, tm=512):
    m = x2d.shape[0]
    n = w.shape[1]
    return pl.pallas_call(
        _inproj_kernel,
        grid=(m // tm, n // tn),
        in_specs=[pl.BlockSpec((tm, D_MODEL), lambda i, j: (i, 0)),
                  pl.BlockSpec((1, D_MODEL), lambda i, j: (0, 0)),
                  pl.BlockSpec((D_MODEL, tn), lambda i, j: (0, j))],
        out_specs=pl.BlockSpec((tm, tn), lambda i, j: (i, j)),
        out_shape=jax.ShapeDtypeStruct((m, n), F32),
        scratch_shapes=[pltpu.VMEM((tm, D_MODEL), BF16)],
        compiler_params=pltpu.CompilerParams(
            dimension_semantics=("parallel", "arbitrary"), vmem_limit_bytes=VMEM_LIMIT),
        name="inproj",
    )(x2d, g, w)


def _unit_lower_inverse(a):
    n = a.shape[0]
    r = lax.broadcasted_iota(jnp.int32, (n, n), 0)
    c = lax.broadcasted_iota(jnp.int32, (n, n), 1)
    same16 = (r >> 4) == (c >> 4)
    same32 = (r >> 5) == (c >> 5)
    a16 = jnp.where(same16, a, 0.0)
    t = jnp.where(r == c, 1.0, 0.0) + a16
    p = _bdot(a16, a16)
    t = t + _bdot(t, p)
    p = _bdot(p, p)
    t = t + _bdot(t, p)
    p = _bdot(p, p)
    t = t + _bdot(t, p)
    b32 = jnp.where(same32, jnp.where(same16, 0.0, a), 0.0)
    t = t + _bdot(_bdot(t, b32), t)
    b64 = jnp.where(same32, 0.0, a)
    t = t + _bdot(_bdot(t, b64), t)
    return t


def _rwkv_kernel(z_ref, mu_ref, wwa_ref, w0a0_ref, gup_ref, vec_ref, seg_ref, tri_ref,
                 y_ref, state_ref, last_ref, *, n_chunk):
    c = CHUNK
    tt = n_chunk * c

    @pl.when(pl.program_id(1) == 0)
    def _():
        state_ref[...] = jnp.zeros_like(state_ref)
        last_ref[...] = jnp.zeros_like(last_ref)

    z = z_ref[0][:, :RW_USED]
    row = lax.broadcasted_iota(jnp.int32, (tt, 1), 0)
    prev = jnp.where(row == 0, last_ref[0:1, :], pltpu.roll(z, 1, axis=0))
    last_ref[0:1, :] = z[tt - 1:tt, :]
    zl = z + mu_ref[...] * (prev - z)

    lane = lax.broadcasted_iota(jnp.int32, (1, LANES), 1)
    first = lane < HEAD_DIM
    wa = zl[:, 3 * WIDTH:3 * WIDTH + LANES]
    wa = jnp.where(first, jnp.tanh(wa), wa)
    wa_out = _dot(wa.astype(BF16), wwa_ref[...]) + w0a0_ref[...]
    gd = zl[:, 3 * WIDTH + LANES:3 * WIDTH + 2 * LANES]
    g_all = _dot(_sigmoid(gd).astype(BF16), gup_ref[...])

    seg = seg_ref[...]
    tri = tri_ref[...]
    r2 = lax.broadcasted_iota(jnp.int32, (2 * c, 2 * c), 0)
    c2 = lax.broadcasted_iota(jnp.int32, (2 * c, 2 * c), 1)
    strict = ((r2 >> 6) == (c2 >> 6)) & ((c2 & (c - 1)) < (r2 & (c - 1)))
    r1 = lax.broadcasted_iota(jnp.int32, (c, 2 * c), 0)
    c1 = lax.broadcasted_iota(jnp.int32, (c, 2 * c), 1)
    incl = (c1 & (c - 1)) <= r1

    def stack(x):
        return jnp.concatenate([jnp.where(first, x, 0.0), jnp.where(first, 0.0, x)], axis=0)

    for p in range(N_PAIR):
        sl = slice(p * LANES, (p + 1) * LANES)
        r_ = zl[:, sl]
        k_ = zl[:, WIDTH + p * LANES:WIDTH + (p + 1) * LANES]
        v_ = zl[:, 2 * WIDTH + p * LANES:2 * WIDTH + (p + 1) * LANES]
        logw = -DECAY_SCALE * _sigmoid(wa_out[:, sl])
        a_ = _sigmoid(wa_out[:, WIDTH + p * LANES:WIDTH + (p + 1) * LANES])
        k_k = vec_ref[0:1, sl]
        k_a = vec_ref[1:2, sl]
        r_k = vec_ref[2:3, sl]
        gn_g = vec_ref[3:4, sl]
        gn_b = vec_ref[4:5, sl]

        kap = k_ * k_k
        kh = kap * lax.rsqrt(_dot_x01(kap * kap, seg) + 1e-12)
        kt = k_ * (1.0 + (a_ - 1.0) * k_a)
        bt = a_ * kh
        bonus = _dot_x01(r_ * kt * r_k, seg) * v_

        cum = _dot_01x(tri, logw)
        e_neg = jnp.exp(-cum)
        qa = -kh * jnp.exp(cum - logw)
        qr = r_ * jnp.exp(cum)
        kb = bt * e_neg
        kk = kt * e_neg

        s = state_ref[p]
        ys = []
        for ch in range(n_chunk):
            rows = slice(ch * c, (ch + 1) * c)
            cum_c = cum[ch * c + c - 1:ch * c + c, :]
            e_rem = jnp.exp(cum_c - cum[rows])
            qa_st = stack(qa[rows]).astype(BF16)
            v_st = stack(v_[rows])
            v_sb = v_st.astype(BF16)
            qr_b = qr[rows].astype(BF16)
            q3 = jnp.concatenate([qa_st, qr_b], axis=0)
            k3 = jnp.concatenate([stack(kb[rows]), stack(kk[rows])], axis=0)
            aa = _dot_nt(q3, k3.astype(BF16))
            a_ab = jnp.where(strict, aa[:2 * c, :2 * c], 0.0)
            a_ak = jnp.where(strict, aa[:2 * c, 2 * c:], 0.0)
            a_rb = jnp.where(incl, aa[2 * c:, :2 * c], 0.0)
            a_rk = jnp.where(incl, aa[2 * c:, 2 * c:], 0.0)

            t_inv = _unit_lower_inverse(a_ab)
            rhs = jnp.concatenate([_dot(a_ak.astype(BF16), v_sb).astype(BF16), qa_st], axis=1)
            tw = _dot(t_inv.astype(BF16), rhs)
            uv_t = tw[:, :LANES].T
            wq_st = tw[:, LANES:].astype(BF16)
            vkk = _bdot(v_st.T, stack(kt[rows] * e_rem))

            s_b = s.astype(BF16)
            u_t = _dot_nt(s_b, wq_st) + uv_t
            u_b = u_t.astype(BF16)
            ys.append(_dot_nt(qr_b, s_b) + _dot_nt(a_rb.astype(BF16), u_b)
                      + _dot(a_rk.astype(BF16), v_sb))
            s = s * jnp.exp(cum_c) + _dot(u_b, stack(bt[rows] * e_rem).astype(BF16)) + vkk
        state_ref[p] = s

        y = jnp.concatenate(ys, axis=0) if n_chunk > 1 else ys[0]
        mean = _dot_x01(y, seg) * (1.0 / HEAD_DIM)
        d = y - mean
        var = _dot_x01(d * d, seg) * (1.0 / HEAD_DIM)
        yn = d * lax.rsqrt(var + GN_EPS) * gn_g + gn_b
        y_ref[0, :, sl] = (yn + bonus) * g_all[:, sl]


def _rwkv(zr, mu, wwa, w0a0, gup, vecs, seg, tri, n_chunk):
    b, t, _ = zr.shape
    tt = n_chunk * CHUNK
    const = lambda shape: pl.BlockSpec(shape, lambda i, j: (0,) * len(shape))
    return pl.pallas_call(
        functools.partial(_rwkv_kernel, n_chunk=n_chunk),
        grid=(b, t // tt),
        in_specs=[pl.BlockSpec((1, tt, RW_COLS), lambda i, j: (i, j, 0)),
                  const((1, RW_USED)), const((LANES, 2 * WIDTH)), const((1, 2 * WIDTH)),
                  const((GATE_RANK, WIDTH)), const((8, WIDTH)), const((LANES, LANES)),
                  const((tt, tt))],
        out_specs=pl.BlockSpec((1, tt, WIDTH), lambda i, j: (i, j, 0)),
        out_shape=jax.ShapeDtypeStruct((b, t, WIDTH), F32),
        scratch_shapes=[pltpu.VMEM((N_PAIR, LANES, LANES), F32),
                        pltpu.VMEM((8, RW_USED), F32)],
        compiler_params=pltpu.CompilerParams(
            dimension_semantics=("parallel", "arbitrary"), vmem_limit_bytes=VMEM_LIMIT),
        name="rwkv7",
    )(zr, mu, wwa, w0a0, gup, vecs, seg, tri)


def _attn_prep_kernel(za_ref, f_ref, b_ref, tri_ref, selq_ref, selk_ref, oneq_ref, onek_ref,
                      q_ref, k_ref, v_ref, carry_ref):
    @pl.when(pl.program_id(1) == 0)
    def _():
        carry_ref[...] = jnp.zeros_like(carry_ref)

    f = f_ref[0] + b_ref[...]
    lf = (jnp.minimum(f, 0.0) - jnp.log(1.0 + jnp.exp(-jnp.abs(f)))) * LOG2E
    cum = _dot_01x(tri_ref[...], lf) + carry_ref[0:1, :]
    n = cum.shape[0]
    carry_ref[0:1, :] = cum[n - 1:n, :]
    parts = jnp.concatenate(_split3(cum), axis=1)
    aq = _dot(parts, selq_ref[...]) + oneq_ref[...]
    ak = _dot(parts, selk_ref[...]) + onek_ref[...]

    lane = lax.broadcasted_iota(jnp.int32, (1, LANES), 1)
    first = lane < HEAD_DIM
    one_lane0 = jnp.where(lane == 0, 1.0, 0.0)
    za = za_ref[0]
    for p in range(N_PAIR):
        q = za[:, p * LANES:(p + 1) * LANES] * (HEAD_DIM ** -0.5 * LOG2E)
        for e in range(2):
            i = 2 * p + e
            q_e = jnp.where(first, q, 0.0) if e == 0 else jnp.where(first, 0.0, q)
            q_ref[0, p, e, :, :LANES] = q_e.astype(BF16)
            q_ref[0, p, e, :, LANES:] = aq[:, i * LANES:(i + 1) * LANES].astype(BF16)
        k_ref[0, p, :, :LANES] = za[:, WIDTH + p * LANES:WIDTH + (p + 1) * LANES].astype(BF16)
        k_ref[0, p, :, LANES:] = ak[:, p * LANES:(p + 1) * LANES].astype(BF16)
        v_ref[0, p, :, :LANES] = za[:, 2 * WIDTH + p * LANES:2 * WIDTH + (p + 1) * LANES].astype(BF16)
        v_ref[0, p, :, LANES:] = jnp.broadcast_to(one_lane0, (n, LANES)).astype(BF16)


def _attn_prep(za, zr, b_f, consts, tf):
    b, t, _ = za.shape
    tri, selq, selk, oneq, onek = consts
    const = lambda shape: pl.BlockSpec(shape, lambda i, j: (0,) * len(shape))
    return pl.pallas_call(
        _attn_prep_kernel,
        grid=(b, t // tf),
        in_specs=[pl.BlockSpec((1, tf, AT_COLS), lambda i, j: (i, j, 0)),
                  pl.BlockSpec((1, tf, LANES), lambda i, j: (i, j, RW_USED // LANES)),
                  const((1, LANES)), const((tf, tf)), const((3 * LANES, 8 * LANES)),
                  const((3 * LANES, 4 * LANES)), const((1, 8 * LANES)), const((1, 4 * LANES))],
        out_specs=[pl.BlockSpec((1, N_PAIR, 2, tf, 2 * LANES), lambda i, j: (i, 0, 0, j, 0)),
                   pl.BlockSpec((1, N_PAIR, tf, 2 * LANES), lambda i, j: (i, 0, j, 0)),
                   pl.BlockSpec((1, N_PAIR, tf, 2 * LANES), lambda i, j: (i, 0, j, 0))],
        out_shape=[jax.ShapeDtypeStruct((b, N_PAIR, 2, t, 2 * LANES), BF16),
                   jax.ShapeDtypeStruct((b, N_PAIR, t, 2 * LANES), BF16),
                   jax.ShapeDtypeStruct((b, N_PAIR, t, 2 * LANES), BF16)],
        scratch_shapes=[pltpu.VMEM((8, LANES), F32)],
        compiler_params=pltpu.CompilerParams(
            dimension_semantics=("parallel", "arbitrary"), vmem_limit_bytes=VMEM_LIMIT),
        name="attn_prep",
    )(za, zr, b_f, tri, selq, selk, oneq, onek)


def _attn_prep_constants(tf):
    selq = np.zeros((3 * LANES, 8 * LANES), np.float32)
    selk = np.zeros((3 * LANES, N_PAIR * LANES), np.float32)
    oneq = np.zeros((1, 8 * LANES), np.float32)
    onek = np.zeros((1, N_PAIR * LANES), np.float32)
    for p in range(N_PAIR):
        for e in range(2):
            h = 2 * p + e
            for part in range(3):
                selq[part * LANES + h, h * LANES + 8 * e + part] = 1.0
                oneq[0, h * LANES + 8 * e + 3 + part] = 1.0
                selk[part * LANES + h, p * LANES + 8 * e + 3 + part] = -1.0
                onek[0, p * LANES + 8 * e + part] = 1.0
    tri = np.tril(np.ones((tf, tf), np.float32))
    return (jnp.asarray(tri, BF16), jnp.asarray(selq, BF16), jnp.asarray(selk, BF16),
            jnp.asarray(oneq), jnp.asarray(onek))


def _attn_kernel(q_ref, k_ref, v_ref, o_ref, m_ref, acc_ref, *, tq):
    i = pl.program_id(2)
    q = q_ref[0, 0].reshape(2 * tq, 2 * LANES)
    m_ref[...] = jnp.full_like(m_ref, NEG_BIG)
    acc_ref[...] = jnp.zeros_like(acc_ref)

    def block(j, masked):
        start = pl.multiple_of(j * tq, tq)
        s = _dot_nt(q, k_ref[0, 0, pl.ds(start, tq), :])
        if masked:
            row = lax.broadcasted_iota(jnp.int32, (2 * tq, tq), 0) & (tq - 1)
            col = lax.broadcasted_iota(jnp.int32, (2 * tq, tq), 1)
            s = jnp.where(col <= row, s, NEG_BIG)
        m_prev = m_ref[...]
        m_new = jnp.maximum(m_prev, jnp.max(s, axis=1, keepdims=True))
        p = jnp.exp2(s - m_new)
        acc_ref[...] = (jnp.exp2(m_prev - m_new) * acc_ref[...]
                        + _dot(p.astype(BF16), v_ref[0, 0, pl.ds(start, tq), :]))
        m_ref[...] = m_new

    def body(j, carry):
        block(j, False)
        return carry

    lax.fori_loop(0, i, body, 0)
    block(i, True)

    acc = acc_ref[...]
    o = acc[:, :LANES] / acc[:, LANES:LANES + 1]
    lane = lax.broadcasted_iota(jnp.int32, (1, LANES), 1)
    o_ref[0] = jnp.where(lane < HEAD_DIM, o[:tq], o[tq:])


def _attention(q_aug, k_aug, v_aug, tq):
    b, _, _, t, _ = q_aug.shape
    return pl.pallas_call(
        functools.partial(_attn_kernel, tq=tq),
        grid=(b, N_PAIR, t // tq),
        in_specs=[pl.BlockSpec((1, 1, 2, tq, 2 * LANES), lambda bi, p, i: (bi, p, 0, i, 0)),
                  pl.BlockSpec((1, 1, t, 2 * LANES), lambda bi, p, i: (bi, p, 0, 0)),
                  pl.BlockSpec((1, 1, t, 2 * LANES), lambda bi, p, i: (bi, p, 0, 0))],
        out_specs=pl.BlockSpec((1, tq, LANES), lambda bi, p, i: (bi, i, p)),
        out_shape=jax.ShapeDtypeStruct((b, t, WIDTH), F32),
        scratch_shapes=[pltpu.VMEM((2 * tq, 1), F32),
                        pltpu.VMEM((2 * tq, 2 * LANES), F32)],
        compiler_params=pltpu.CompilerParams(
            dimension_semantics=("parallel", "parallel", "arbitrary"),
            vmem_limit_bytes=VMEM_LIMIT),
        name="forget_attention",
    )(q_aug, k_aug, v_aug)


def _shift_rows(u, carry_ref, tm):
    row = lax.broadcasted_iota(jnp.int32, (tm, 1), 0)
    prev1 = carry_ref[1:2, :]
    prev2 = carry_ref[0:1, :]
    u1 = jnp.where(row == 0, prev1, pltpu.roll(u, 1, axis=0))
    u2 = jnp.where(row == 0, prev2, jnp.where(row == 1, prev1, pltpu.roll(u, 2, axis=0)))
    carry_ref[0:2, :] = u[tm - 2:tm, :]
    return u1, u2


def _merge_kernel(zg_ref, zc_ref, yb_ref, yc_ref, x_ref, cw_ref, gb_ref, wb_ref, wo_ref,
                  o_ref, carry_ref, *, tm):
    @pl.when(pl.program_id(1) == 0)
    def _():
        carry_ref[...] = jnp.zeros_like(carry_ref)

    zc = zc_ref[0]
    u = zc[:, WIDTH:2 * WIDTH] * zc[:, 2 * WIDTH:]
    u1, u2 = _shift_rows(u, carry_ref, tm)
    ya = zc[:, :WIDTH] * (cw_ref[0:1, :] * u2 + cw_ref[1:2, :] * u1 + cw_ref[2:3, :] * u)
    gates = _sigmoid(zg_ref[0] + gb_ref[...])
    merged = (gates[:, :D_MODEL] * _dot(ya.astype(BF16), wb_ref[0])
              + gates[:, D_MODEL:2 * D_MODEL] * _dot(yb_ref[0].astype(BF16), wb_ref[1])
              + gates[:, 2 * D_MODEL:] * _dot(yc_ref[0].astype(BF16), wb_ref[2]))
    o_ref[0] = x_ref[0] + _dot(merged.astype(BF16), wo_ref[...])


def _merge(zgc, yb, yc, x3d, conv_w, gate_b, w_branch, w_o, tm=256):
    b, t, _ = zgc.shape
    const = lambda shape: pl.BlockSpec(shape, lambda i, j: (0,) * len(shape))
    tile = lambda w, col: pl.BlockSpec((1, tm, w), lambda i, j: (i, j, col))
    return pl.pallas_call(
        functools.partial(_merge_kernel, tm=tm),
        grid=(b, t // tm),
        in_specs=[tile(3 * D_MODEL, 0), tile(3 * WIDTH, 3 * D_MODEL // (3 * WIDTH)),
                  tile(WIDTH, 0), tile(WIDTH, 0), tile(D_MODEL, 0),
                  const((8, WIDTH)), const((1, 3 * D_MODEL)),
                  const((3, WIDTH, D_MODEL)), const((D_MODEL, D_MODEL))],
        out_specs=tile(D_MODEL, 0),
        out_shape=jax.ShapeDtypeStruct((b, t, D_MODEL), F32),
        scratch_shapes=[pltpu.VMEM((8, WIDTH), F32)],
        compiler_params=pltpu.CompilerParams(
            dimension_semantics=("parallel", "arbitrary"), vmem_limit_bytes=VMEM_LIMIT),
        name="merge",
    )(zgc, zgc, yb, yc, x3d, conv_w, gate_b, w_branch, w_o)


def _ffn_kernel(x_ref, g_ref, wg_ref, wu_ref, cg_ref, cu_ref, wd_ref, fg_ref, o_ref,
                xn_ref, acc_ref, carry_g_ref, carry_u_ref, *, tm, final):
    f = pl.program_id(2)
    nf = pl.num_programs(2)

    @pl.when((pl.program_id(1) == 0) & (f == 0))
    def _():
        carry_g_ref[...] = jnp.zeros_like(carry_g_ref)
        carry_u_ref[...] = jnp.zeros_like(carry_u_ref)

    @pl.when(f == 0)
    def _():
        x = x_ref[0]
        ms = jnp.mean(x * x, axis=-1, keepdims=True)
        xn_ref[...] = (x * lax.rsqrt(ms + NORM_EPS) * g_ref[...]).astype(BF16)
        acc_ref[...] = jnp.zeros_like(acc_ref)

    xn = xn_ref[...]

    def conv(h, carry_ref, cw_ref):
        h1, h2 = _shift_rows(h, carry_ref.at[f], tm)
        return cw_ref[0:1, :] * h2 + cw_ref[1:2, :] * h1 + cw_ref[2:3, :] * h

    hg = conv(_dot(xn, wg_ref[...]), carry_g_ref, cg_ref)
    hu = conv(_dot(xn, wu_ref[...]), carry_u_ref, cu_ref)
    act = hg * _sigmoid(hg) * hu
    acc_ref[...] += _dot(act.astype(BF16), wd_ref[...])

    @pl.when(f == nf - 1)
    def _():
        y = x_ref[0] + acc_ref[...]
        if final:
            ms = jnp.mean(y * y, axis=-1, keepdims=True)
            y = y * lax.rsqrt(ms + NORM_EPS) * fg_ref[...]
        o_ref[0] = y


def _ffn(x3d, g, w_up, conv_w, w_down, final_g, final, tm=512, fc=1408):
    b, t, _ = x3d.shape
    nf = D_FF // fc
    return pl.pallas_call(
        functools.partial(_ffn_kernel, tm=tm, final=final),
        grid=(b, t // tm, nf),
        in_specs=[pl.BlockSpec((1, tm, D_MODEL), lambda i, j, f: (i, j, 0)),
                  pl.BlockSpec((1, D_MODEL), lambda i, j, f: (0, 0)),
                  pl.BlockSpec((D_MODEL, fc), lambda i, j, f: (0, f)),
                  pl.BlockSpec((D_MODEL, fc), lambda i, j, f: (0, f + nf)),
                  pl.BlockSpec((8, fc), lambda i, j, f: (0, f)),
                  pl.BlockSpec((8, fc), lambda i, j, f: (0, f + nf)),
                  pl.BlockSpec((fc, D_MODEL), lambda i, j, f: (f, 0)),
                  pl.BlockSpec((1, D_MODEL), lambda i, j, f: (0, 0))],
        out_specs=pl.BlockSpec((1, tm, D_MODEL), lambda i, j, f: (i, j, 0)),
        out_shape=jax.ShapeDtypeStruct((b, t, D_MODEL), F32),
        scratch_shapes=[pltpu.VMEM((tm, D_MODEL), BF16),
                        pltpu.VMEM((tm, D_MODEL), F32),
                        pltpu.VMEM((nf, 8, fc), F32),
                        pltpu.VMEM((nf, 8, fc), F32)],
        compiler_params=pltpu.CompilerParams(
            dimension_semantics=("parallel", "arbitrary", "arbitrary"),
            vmem_limit_bytes=VMEM_LIMIT),
        name="conv_ffn",
    )(x3d, g, w_up, w_up, conv_w, conv_w, w_down, final_g)


def _pad_rows(a, rows=8):
    return jnp.pad(a, ((0, rows - a.shape[0]), (0, 0)))


def _pack_w_in(w):
    conv_cols = 3 * WIDTH
    rwkv_cols = 3 * WIDTH + DECAY_RANK + ICLR_RANK + GATE_RANK
    n_heads = WIDTH // HEAD_DIM
    o1 = conv_cols
    o2 = o1 + rwkv_cols
    o3 = o2 + AT_COLS
    o4 = o3 + n_heads
    w_gc = jnp.concatenate([w[:, o4:], w[:, :o1]], axis=1)
    pad = jnp.zeros((w.shape[0], RW_COLS - RW_USED - n_heads), w.dtype)
    w_rw = jnp.concatenate([w[:, o1:o2], w[:, o3:o4], pad], axis=1)
    return w_gc.astype(BF16), w_rw.astype(BF16), w[:, o2:o3].astype(BF16)


def kernel(x, norm1_g, w_in, gate_b, conv_mix_w, rwkv_mu, rwkv_w0, rwkv_w_up, rwkv_a0,
           rwkv_a_up, rwkv_g_up, rwkv_k_k, rwkv_k_a, rwkv_r_k, rwkv_gn_g, rwkv_gn_b,
           attn_forget_b, w_branch, w_o, norm2_g, ffn_w_up, ffn_conv_w, ffn_w_down,
           final_norm_g):
    b, t, d = x.shape
    depth = w_in.shape[0]
    n_heads = WIDTH // HEAD_DIM
    tf = 256
    tq = 512
    n_chunk = 4
    tt = n_chunk * CHUNK

    lane = np.arange(LANES)
    seg = jnp.asarray((lane[:, None] // HEAD_DIM) == (lane[None, :] // HEAD_DIM), BF16)
    pos = np.arange(tt)
    tri_c = jnp.asarray((pos[:, None] >= pos[None, :]) & (pos[:, None] // CHUNK == pos[None, :] // CHUNK),
                        BF16)
    prep_consts = _attn_prep_constants(tf)

    for l in range(depth):
        w_gc, w_rw, w_at = _pack_w_in(w_in[l])
        x2d = x.reshape(b * t, d)
        g1 = norm1_g[l][None, :]
        zgc = _inproj(x2d, g1, w_gc, tn=GC_COLS // 3).reshape(b, t, GC_COLS)
        zr = _inproj(x2d, g1, w_rw, tn=RW_COLS // 2).reshape(b, t, RW_COLS)
        za = _inproj(x2d, g1, w_at, tn=AT_COLS).reshape(b, t, AT_COLS)

        wwa = jnp.zeros((LANES, 2 * WIDTH), F32)
        wwa = wwa.at[:DECAY_RANK, :WIDTH].set(rwkv_w_up[l]).at[DECAY_RANK:, WIDTH:].set(rwkv_a_up[l])
        w0a0 = jnp.concatenate([rwkv_w0[l], rwkv_a0[l]])[None, :]
        vecs = _pad_rows(jnp.stack([rwkv_k_k[l], rwkv_k_a[l], rwkv_r_k[l].reshape(WIDTH),
                                    rwkv_gn_g[l], rwkv_gn_b[l]]))
        y_b = _rwkv(zr, rwkv_mu[l][None, :], wwa.astype(BF16), w0a0,
                    rwkv_g_up[l].astype(BF16), vecs, seg, tri_c, n_chunk)

        b_f = jnp.pad(attn_forget_b[l], (0, LANES - n_heads))[None, :]
        q_aug, k_aug, v_aug = _attn_prep(za, zr, b_f, prep_consts, tf)
        y_c = _attention(q_aug, k_aug, v_aug, tq)

        x = _merge(zgc, y_b, y_c, x, _pad_rows(conv_mix_w[l]), gate_b[l][None, :],
                   w_branch[l].astype(BF16), w_o[l].astype(BF16))
        x = _ffn(x, norm2_g[l][None, :], ffn_w_up[l].astype(BF16), _pad_rows(ffn_conv_w[l]),
                 ffn_w_down[l].astype(BF16), final_norm_g[None, :], final=(l == depth - 1))
    return x
```

```python
import functools
import math

import numpy as np
import jax
import jax.numpy as jnp
from jax import lax
from jax.experimental import pallas as pl
from jax.experimental.pallas import tpu as pltpu

F32 = jnp.float32
BF16 = jnp.bfloat16

D_MODEL = 1024
HEAD_DIM = 64
WIDTH = 512
N_PAIR = WIDTH // (2 * HEAD_DIM)
LANES = 128
DECAY_RANK = 64
ICLR_RANK = 64
GATE_RANK = 128
D_FF = 2816
NORM_EPS = 1e-6
GN_EPS = 64e-5
DECAY_SCALE = math.exp(-0.5)
CHUNK = 64
LOG2E = math.log2(math.e)
NEG_BIG = -1e30

GC_COLS = 3 * D_MODEL + 3 * WIDTH
RW_COLS = 2048
RW_USED = 1792
AT_COLS = 3 * WIDTH

VMEM_LIMIT = 56 * 1024 * 1024


def _dot(a, b):
    return jnp.dot(a, b, preferred_element_type=F32)


def _dot_nt(a, b):
    return lax.dot_general(a, b, (((1,), (1,)), ((), ())), preferred_element_type=F32)


def _bdot(a, b):
    return _dot(a.astype(BF16), b.astype(BF16))


def _bdot_nt(a, b):
    return _dot_nt(a.astype(BF16), b.astype(BF16))


def _split3(x):
    hi = x.astype(BF16)
    r1 = x - hi.astype(F32)
    mid = r1.astype(BF16)
    lo = (r1 - mid.astype(F32)).astype(BF16)
    return hi, mid, lo


def _dot_x01(x, m01):
    hi, mid, lo = _split3(x)
    return _dot(hi, m01) + _dot(mid, m01) + _dot(lo, m01)


def _dot_01x(m01, x):
    hi, mid, lo = _split3(x)
    return _dot(m01, hi) + _dot(m01, mid) + _dot(m01, lo)


def _sigmoid(x):
    return 1.0 / (1.0 + jnp.exp(-x))


def _inproj_kernel(x_ref, g_ref, w_ref, o_ref, xn_ref):
    @pl.when(pl.program_id(1) == 0)
    def _():
        x = x_ref[...]
        ms = jnp.mean(x * x, axis=-1, keepdims=True)
        xn_ref[...] = (x * lax.rsqrt(ms + NORM_EPS) * g_ref[...]).astype(BF16)

    o_ref[...] = _dot(xn_ref[...], w_ref[...])


def _inproj(x2d, g, w, tn, tm=512):
    m = x2d.shape[0]
    n = w.shape[1]
    return pl.pallas_call(
        _inproj_kernel,
        grid=(m // tm, n // tn),
        in_specs=[pl.BlockSpec((tm, D_MODEL), lambda i, j: (i, 0)),
                  pl.BlockSpec((1, D_MODEL), lambda i, j: (0, 0)),
                  pl.BlockSpec((D_MODEL, tn), lambda i, j: (0, j))],
        out_specs=pl.BlockSpec((tm, tn), lambda i, j: (i, j)),
        out_shape=jax.ShapeDtypeStruct((m, n), F32),
        scratch_shapes=[pltpu.VMEM((tm, D_MODEL), BF16)],
        compiler_params=pltpu.CompilerParams(
            dimension_semantics=("parallel", "arbitrary"), vmem_limit_bytes=VMEM_LIMIT),
        name="inproj",
    )(x2d, g, w)


def _unit_lower_inverses(mats):
    n = mats[0].shape[0]
    r = lax.broadcasted_iota(jnp.int32, (n, n), 0)
    c = lax.broadcasted_iota(jnp.int32, (n, n), 1)
    same16 = (r >> 4) == (c >> 4)
    same32 = (r >> 5) == (c >> 5)
    eye = jnp.where(r == c, 1.0, 0.0)
    a16 = [jnp.where(same16, a, 0.0).astype(BF16) for a in mats]
    ts = [eye + a.astype(F32) for a in a16]
    ps = [_dot(a, a) for a in a16]
    for step in range(3):
        pb = [p.astype(BF16) for p in ps]
        ts = [t + _dot(t.astype(BF16), p) for t, p in zip(ts, pb)]
        if step < 2:
            ps = [_dot(p, p) for p in pb]
    for blocks in (lambda a: jnp.where(same32, jnp.where(same16, 0.0, a), 0.0),
                   lambda a: jnp.where(same32, 0.0, a)):
        tb = [t.astype(BF16) for t in ts]
        mid = [_dot(t, blocks(a).astype(BF16)).astype(BF16) for t, a in zip(tb, mats)]
        ts = [t + _dot(m, t_b) for t, m, t_b in zip(ts, mid, tb)]
    return ts


def _rwkv_kernel(z_ref, mu_ref, wwa_ref, w0a0_ref, gup_ref, vec_ref, seg_ref, tri_ref,
                 y_ref, state_ref, last_ref, *, n_chunk):
    c = CHUNK
    tt = n_chunk * c

    @pl.when(pl.program_id(1) == 0)
    def _():
        state_ref[...] = jnp.zeros_like(state_ref)
        last_ref[...] = jnp.zeros_like(last_ref)

    z = z_ref[0][:, :RW_USED]
    row = lax.broadcasted_iota(jnp.int32, (tt, 1), 0)
    prev = jnp.where(row == 0, last_ref[0:1, :], pltpu.roll(z, 1, axis=0))
    last_ref[0:1, :] = z[tt - 1:tt, :]
    zl = z + mu_ref[...] * (prev - z)

    lane = lax.broadcasted_iota(jnp.int32, (1, LANES), 1)
    first = lane < HEAD_DIM
    wa = zl[:, 3 * WIDTH:3 * WIDTH + LANES]
    wa = jnp.where(first, jnp.tanh(wa), wa)
    wa_out = _dot(wa.astype(BF16), wwa_ref[...]) + w0a0_ref[...]
    gd = zl[:, 3 * WIDTH + LANES:3 * WIDTH + 2 * LANES]
    g_all = _dot(_sigmoid(gd).astype(BF16), gup_ref[...])

    seg = seg_ref[...]
    tri = tri_ref[...]
    r2 = lax.broadcasted_iota(jnp.int32, (2 * c, 2 * c), 0)
    c2 = lax.broadcasted_iota(jnp.int32, (2 * c, 2 * c), 1)
    strict = ((r2 >> 6) == (c2 >> 6)) & ((c2 & (c - 1)) < (r2 & (c - 1)))
    r1 = lax.broadcasted_iota(jnp.int32, (c, 2 * c), 0)
    c1 = lax.broadcasted_iota(jnp.int32, (c, 2 * c), 1)
    incl = (c1 & (c - 1)) <= r1

    def stack(x):
        return jnp.concatenate([jnp.where(first, x, 0.0), jnp.where(first, 0.0, x)], axis=0)

    pairs = range(N_PAIR)
    chunks = range(n_chunk)
    lanes = [slice(p * LANES, (p + 1) * LANES) for p in pairs]
    r_ = [zl[:, sl] for sl in lanes]
    k_ = [zl[:, WIDTH + p * LANES:WIDTH + (p + 1) * LANES] for p in pairs]
    v_ = [zl[:, 2 * WIDTH + p * LANES:2 * WIDTH + (p + 1) * LANES] for p in pairs]
    logw = [-DECAY_SCALE * _sigmoid(wa_out[:, sl]) for sl in lanes]
    a_ = [_sigmoid(wa_out[:, WIDTH + p * LANES:WIDTH + (p + 1) * LANES]) for p in pairs]
    kap = [k_[p] * vec_ref[0:1, lanes[p]] for p in pairs]
    kss = [_dot_x01(kap[p] * kap[p], seg) for p in pairs]
    kh = [kap[p] * lax.rsqrt(kss[p] + 1e-12) for p in pairs]
    kt = [k_[p] * (1.0 + (a_[p] - 1.0) * vec_ref[1:2, lanes[p]]) for p in pairs]
    bt = [a_[p] * kh[p] for p in pairs]
    rks = [_dot_x01(r_[p] * kt[p] * vec_ref[2:3, lanes[p]], seg) for p in pairs]
    cum = [_dot_01x(tri, logw[p]) for p in pairs]
    e_neg = [jnp.exp(-cum[p]) for p in pairs]
    qa = [-kh[p] * jnp.exp(cum[p] - logw[p]) for p in pairs]
    qr = [(r_[p] * jnp.exp(cum[p])).astype(BF16) for p in pairs]
    kb = [bt[p] * e_neg[p] for p in pairs]
    kk = [kt[p] * e_neg[p] for p in pairs]

    pc = [(p, ch) for p in pairs for ch in chunks]
    rows = [slice(ch * c, (ch + 1) * c) for ch in chunks]
    cum_c = {(p, ch): cum[p][ch * c + c - 1:ch * c + c, :] for p, ch in pc}
    e_rem = {k: jnp.exp(cum_c[k] - cum[k[0]][rows[k[1]]]) for k in pc}
    qa_st = {(p, ch): stack(qa[p][rows[ch]]).astype(BF16) for p, ch in pc}
    v_st = {(p, ch): stack(v_[p][rows[ch]]) for p, ch in pc}
    v_sb = {k: v_st[k].astype(BF16) for k in pc}
    aa = {}
    for p, ch in pc:
        q3 = jnp.concatenate([qa_st[p, ch], qr[p][rows[ch]]], axis=0)
        k3 = jnp.concatenate([stack(kb[p][rows[ch]]), stack(kk[p][rows[ch]])], axis=0)
        aa[p, ch] = _dot_nt(q3, k3.astype(BF16))
    a_ab = [jnp.where(strict, aa[k][:2 * c, :2 * c], 0.0) for k in pc]
    a_ak = {k: jnp.where(strict, aa[k][:2 * c, 2 * c:], 0.0).astype(BF16) for k in pc}
    a_rb = {k: jnp.where(incl, aa[k][2 * c:, :2 * c], 0.0).astype(BF16) for k in pc}
    a_rk = {k: jnp.where(incl, aa[k][2 * c:, 2 * c:], 0.0).astype(BF16) for k in pc}

    t_inv = dict(zip(pc, _unit_lower_inverses(a_ab)))
    akv = {k: _dot(a_ak[k], v_sb[k]).astype(BF16) for k in pc}
    tw = {k: _dot(t_inv[k].astype(BF16), jnp.concatenate([akv[k], qa_st[k]], axis=1)) for k in pc}
    uv_t = {k: tw[k][:, :LANES].T for k in pc}
    wq_st = {k: tw[k][:, LANES:].astype(BF16) for k in pc}
    vkk = {(p, ch): _bdot(v_st[p, ch].T, stack(kt[p][rows[ch]] * e_rem[p, ch])) for p, ch in pc}
    kb2 = {(p, ch): stack(bt[p][rows[ch]] * e_rem[p, ch]).astype(BF16) for p, ch in pc}
    ark_v = {k: _dot(a_rk[k], v_sb[k]) for k in pc}
    decay_c = {k: jnp.exp(cum_c[k]) for k in pc}

    s = [state_ref[p] for p in pairs]
    ys = {}
    for ch in chunks:
        s_b = [s[p].astype(BF16) for p in pairs]
        u_b = [(_dot_nt(s_b[p], wq_st[p, ch]) + uv_t[p, ch]).astype(BF16) for p in pairs]
        s = [s[p] * decay_c[p, ch] + _dot(u_b[p], kb2[p, ch]) + vkk[p, ch] for p in pairs]
        for p in pairs:
            ys[p, ch] = (_dot_nt(qr[p][rows[ch]], s_b[p]) + _dot_nt(a_rb[p, ch], u_b[p])
                         + ark_v[p, ch])
    for p in pairs:
        state_ref[p] = s[p]

    for p in pairs:
        sl = lanes[p]
        y = jnp.concatenate([ys[p, ch] for ch in chunks], axis=0) if n_chunk > 1 else ys[p, 0]
        mean = _dot_x01(y, seg) * (1.0 / HEAD_DIM)
        d = y - mean
        var = _dot_x01(d * d, seg) * (1.0 / HEAD_DIM)
        yn = d * lax.rsqrt(var + GN_EPS) * vec_ref[3:4, sl] + vec_ref[4:5, sl]
        y_ref[0, :, sl] = (yn + rks[p] * v_[p]) * g_all[:, sl]


def _rwkv(zr, mu, wwa, w0a0, gup, vecs, seg, tri, n_chunk):
    b, t, _ = zr.shape
    tt = n_chunk * CHUNK
    const = lambda shape: pl.BlockSpec(shape, lambda i, j: (0,) * len(shape))
    return pl.pallas_call(
        functools.partial(_rwkv_kernel, n_chunk=n_chunk),
        grid=(b, t // tt),
        in_specs=[pl.BlockSpec((1, tt, RW_COLS), lambda i, j: (i, j, 0)),
                  const((1, RW_USED)), const((LANES, 2 * WIDTH)), const((1, 2 * WIDTH)),
                  const((GATE_RANK, WIDTH)), const((8, WIDTH)), const((LANES, LANES)),
                  const((tt, tt))],
        out_specs=pl.BlockSpec((1, tt, WIDTH), lambda i, j: (i, j, 0)),
        out_shape=jax.ShapeDtypeStruct((b, t, WIDTH), F32),
        scratch_shapes=[pltpu.VMEM((N_PAIR, LANES, LANES), F32),
                        pltpu.VMEM((8, RW_USED), F32)],
        compiler_params=pltpu.CompilerParams(
            dimension_semantics=("parallel", "arbitrary"), vmem_limit_bytes=VMEM_LIMIT),
        name="rwkv7",
    )(zr, mu, wwa, w0a0, gup, vecs, seg, tri)


def _attn_prep_kernel(za_ref, f_ref, b_ref, tri_ref, selq_ref, selk_ref, oneq_ref, onek_ref,
                      q_ref, k_ref, v_ref, carry_ref):
    @pl.when(pl.program_id(1) == 0)
    def _():
        carry_ref[...] = jnp.zeros_like(carry_ref)

    f = f_ref[0] + b_ref[...]
    lf = (jnp.minimum(f, 0.0) - jnp.log(1.0 + jnp.exp(-jnp.abs(f)))) * LOG2E
    cum = _dot_01x(tri_ref[...], lf) + carry_ref[0:1, :]
    n = cum.shape[0]
    carry_ref[0:1, :] = cum[n - 1:n, :]
    parts = jnp.concatenate(_split3(cum), axis=1)
    aq = _dot(parts, selq_ref[...]) + oneq_ref[...]
    ak = _dot(parts, selk_ref[...]) + onek_ref[...]

    lane = lax.broadcasted_iota(jnp.int32, (1, LANES), 1)
    first = lane < HEAD_DIM
    one_lane0 = jnp.where(lane == 0, 1.0, 0.0)
    za = za_ref[0]
    for p in range(N_PAIR):
        q = za[:, p * LANES:(p + 1) * LANES] * (HEAD_DIM ** -0.5 * LOG2E)
        for e in range(2):
            i = 2 * p + e
            q_e = jnp.where(first, q, 0.0) if e == 0 else jnp.where(first, 0.0, q)
            q_ref[0, p, e, :, :LANES] = q_e.astype(BF16)
            q_ref[0, p, e, :, LANES:] = aq[:, i * LANES:(i + 1) * LANES].astype(BF16)
        k_ref[0, p, :, :LANES] = za[:, WIDTH + p * LANES:WIDTH + (p + 1) * LANES].astype(BF16)
        k_ref[0, p, :, LANES:] = ak[:, p * LANES:(p + 1) * LANES].astype(BF16)
        v_ref[0, p, :, :LANES] = za[:, 2 * WIDTH + p * LANES:2 * WIDTH + (p + 1) * LANES].astype(BF16)
        v_ref[0, p, :, LANES:] = jnp.broadcast_to(one_lane0, (n, LANES)).astype(BF16)


def _attn_prep(za, zr, b_f, consts, tf):
    b, t, _ = za.shape
    tri, selq, selk, oneq, onek = consts
    const = lambda shape: pl.BlockSpec(shape, lambda i, j: (0,) * len(shape))
    return pl.pallas_call(
        _attn_prep_kernel,
        grid=(b, t // tf),
        in_specs=[pl.BlockSpec((1, tf, AT_COLS), lambda i, j: (i, j, 0)),
                  pl.BlockSpec((1, tf, LANES), lambda i, j: (i, j, RW_USED // LANES)),
                  const((1, LANES)), const((tf, tf)), const((3 * LANES, 8 * LANES)),
                  const((3 * LANES, 4 * LANES)), const((1, 8 * LANES)), const((1, 4 * LANES))],
        out_specs=[pl.BlockSpec((1, N_PAIR, 2, tf, 2 * LANES), lambda i, j: (i, 0, 0, j, 0)),
                   pl.BlockSpec((1, N_PAIR, tf, 2 * LANES), lambda i, j: (i, 0, j, 0)),
                   pl.BlockSpec((1, N_PAIR, tf, 2 * LANES), lambda i, j: (i, 0, j, 0))],
        out_shape=[jax.ShapeDtypeStruct((b, N_PAIR, 2, t, 2 * LANES), BF16),
                   jax.ShapeDtypeStruct((b, N_PAIR, t, 2 * LANES), BF16),
                   jax.ShapeDtypeStruct((b, N_PAIR, t, 2 * LANES), BF16)],
        scratch_shapes=[pltpu.VMEM((8, LANES), F32)],
        compiler_params=pltpu.CompilerParams(
            dimension_semantics=("parallel", "arbitrary"), vmem_limit_bytes=VMEM_LIMIT),
        name="attn_prep",
    )(za, zr, b_f, tri, selq, selk, oneq, onek)


def _attn_prep_constants(tf):
    selq = np.zeros((3 * LANES, 8 * LANES), np.float32)
    selk = np.zeros((3 * LANES, N_PAIR * LANES), np.float32)
    oneq = np.zeros((1, 8 * LANES), np.float32)
    onek = np.zeros((1, N_PAIR * LANES), np.float32)
    for p in range(N_PAIR):
        for e in range(2):
            h = 2 * p + e
            for part in range(3):
                selq[part * LANES + h, h * LANES + 8 * e + part] = 1.0
                oneq[0, h * LANES + 8 * e + 3 + part] = 1.0
                selk[part * LANES + h, p * LANES + 8 * e + 3 + part] = -1.0
                onek[0, p * LANES + 8 * e + part] = 1.0
    tri = np.tril(np.ones((tf, tf), np.float32))
    return (jnp.asarray(tri, BF16), jnp.asarray(selq, BF16), jnp.asarray(selk, BF16),
            jnp.asarray(oneq), jnp.asarray(onek))


def _attn_kernel(q_ref, k_ref, v_ref, o_ref, m_ref, acc_ref, sa_ref, sb_ref, *, tq):
    i = pl.program_id(2)
    q = q_ref[0, 0].reshape(2 * tq, 2 * LANES)
    m_ref[...] = jnp.full_like(m_ref, NEG_BIG)
    acc_ref[...] = jnp.zeros_like(acc_ref)

    def scores(j, s_ref):
        start = pl.multiple_of(j * tq, tq)
        s_ref[...] = _dot_nt(q, k_ref[0, 0, pl.ds(start, tq), :])

    def block(j, s_ref, masked):
        start = pl.multiple_of(j * tq, tq)
        s = s_ref[...]
        if masked:
            row = lax.broadcasted_iota(jnp.int32, (2 * tq, tq), 0) & (tq - 1)
            col = lax.broadcasted_iota(jnp.int32, (2 * tq, tq), 1)
            s = jnp.where(col <= row, s, NEG_BIG)
        m_prev = m_ref[...]
        m_new = jnp.maximum(m_prev, jnp.max(s, axis=1, keepdims=True))
        p = jnp.exp2(s - m_new)
        acc_ref[...] = (jnp.exp2(m_prev - m_new) * acc_ref[...]
                        + _dot(p.astype(BF16), v_ref[0, 0, pl.ds(start, tq), :]))
        m_ref[...] = m_new

    scores(0, sa_ref)

    def body(jj, carry):
        scores(2 * jj + 1, sb_ref)
        block(2 * jj, sa_ref, False)
        scores(2 * jj + 2, sa_ref)
        block(2 * jj + 1, sb_ref, False)
        return carry

    lax.fori_loop(0, i // 2, body, 0)

    @pl.when(i % 2 == 0)
    def _():
        block(i, sa_ref, True)

    @pl.when(i % 2 == 1)
    def _():
        scores(i, sb_ref)
        block(i - 1, sa_ref, False)
        block(i, sb_ref, True)

    acc = acc_ref[...]
    o = acc[:, :LANES] / acc[:, LANES:LANES + 1]
    lane = lax.broadcasted_iota(jnp.int32, (1, LANES), 1)
    o_ref[0] = jnp.where(lane < HEAD_DIM, o[:tq], o[tq:])


def _attention(q_aug, k_aug, v_aug, tq):
    b, _, _, t, _ = q_aug.shape
    return pl.pallas_call(
        functools.partial(_attn_kernel, tq=tq),
        grid=(b, N_PAIR, t // tq),
        in_specs=[pl.BlockSpec((1, 1, 2, tq, 2 * LANES), lambda bi, p, i: (bi, p, 0, i, 0)),
                  pl.BlockSpec((1, 1, t, 2 * LANES), lambda bi, p, i: (bi, p, 0, 0)),
                  pl.BlockSpec((1, 1, t, 2 * LANES), lambda bi, p, i: (bi, p, 0, 0))],
        out_specs=pl.BlockSpec((1, tq, LANES), lambda bi, p, i: (bi, i, p)),
        out_shape=jax.ShapeDtypeStruct((b, t, WIDTH), F32),
        scratch_shapes=[pltpu.VMEM((2 * tq, 1), F32),
                        pltpu.VMEM((2 * tq, 2 * LANES), F32),
                        pltpu.VMEM((2 * tq, tq), F32),
                        pltpu.VMEM((2 * tq, tq), F32)],
        compiler_params=pltpu.CompilerParams(
            dimension_semantics=("parallel", "parallel", "arbitrary"),
            vmem_limit_bytes=VMEM_LIMIT),
        name="forget_attention",
    )(q_aug, k_aug, v_aug)


def _shift_rows(u, carry_ref, tm):
    row = lax.broadcasted_iota(jnp.int32, (tm, 1), 0)
    prev1 = carry_ref[1:2, :]
    prev2 = carry_ref[0:1, :]
    u1 = jnp.where(row == 0, prev1, pltpu.roll(u, 1, axis=0))
    u2 = jnp.where(row == 0, prev2, jnp.where(row == 1, prev1, pltpu.roll(u, 2, axis=0)))
    carry_ref[0:2, :] = u[tm - 2:tm, :]
    return u1, u2


def _merge_kernel(zg_ref, zc_ref, yb_ref, yc_ref, x_ref, cw_ref, gb_ref, wb_ref, wo_ref,
                  o_ref, carry_ref, *, tm):
    @pl.when(pl.program_id(1) == 0)
    def _():
        carry_ref[...] = jnp.zeros_like(carry_ref)

    zc = zc_ref[0]
    u = zc[:, WIDTH:2 * WIDTH] * zc[:, 2 * WIDTH:]
    u1, u2 = _shift_rows(u, carry_ref, tm)
    ya = zc[:, :WIDTH] * (cw_ref[0:1, :] * u2 + cw_ref[1:2, :] * u1 + cw_ref[2:3, :] * u)
    gates = _sigmoid(zg_ref[0] + gb_ref[...])
    merged = (gates[:, :D_MODEL] * _dot(ya.astype(BF16), wb_ref[0])
              + gates[:, D_MODEL:2 * D_MODEL] * _dot(yb_ref[0].astype(BF16), wb_ref[1])
              + gates[:, 2 * D_MODEL:] * _dot(yc_ref[0].astype(BF16), wb_ref[2]))
    o_ref[0] = x_ref[0] + _dot(merged.astype(BF16), wo_ref[...])


def _merge(zgc, yb, yc, x3d, conv_w, gate_b, w_branch, w_o, tm=256):
    b, t, _ = zgc.shape
    const = lambda shape: pl.BlockSpec(shape, lambda i, j: (0,) * len(shape))
    tile = lambda w, col: pl.BlockSpec((1, tm, w), lambda i, j: (i, j, col))
    return pl.pallas_call(
        functools.partial(_merge_kernel, tm=tm),
        grid=(b, t // tm),
        in_specs=[tile(3 * D_MODEL, 0), tile(3 * WIDTH, 3 * D_MODEL // (3 * WIDTH)),
                  tile(WIDTH, 0), tile(WIDTH, 0), tile(D_MODEL, 0),
                  const((8, WIDTH)), const((1, 3 * D_MODEL)),
                  const((3, WIDTH, D_MODEL)), const((D_MODEL, D_MODEL))],
        out_specs=tile(D_MODEL, 0),
        out_shape=jax.ShapeDtypeStruct((b, t, D_MODEL), F32),
        scratch_shapes=[pltpu.VMEM((8, WIDTH), F32)],
        compiler_params=pltpu.CompilerParams(
            dimension_semantics=("parallel", "arbitrary"), vmem_limit_bytes=VMEM_LIMIT),
        name="merge",
    )(zgc, zgc, yb, yc, x3d, conv_w, gate_b, w_branch, w_o)


def _ffn_kernel(x_ref, g_ref, wg_ref, wu_ref, cg_ref, cu_ref, wd_ref, fg_ref, o_ref,
                xn_ref, acc_ref, carry_g_ref, carry_u_ref, *, tm, final):
    f = pl.program_id(2)
    nf = pl.num_programs(2)

    @pl.when((pl.program_id(1) == 0) & (f == 0))
    def _():
        carry_g_ref[...] = jnp.zeros_like(carry_g_ref)
        carry_u_ref[...] = jnp.zeros_like(carry_u_ref)

    @pl.when(f == 0)
    def _():
        x = x_ref[0]
        ms = jnp.mean(x * x, axis=-1, keepdims=True)
        xn_ref[...] = (x * lax.rsqrt(ms + NORM_EPS) * g_ref[...]).astype(BF16)
        acc_ref[...] = jnp.zeros_like(acc_ref)

    xn = xn_ref[...]

    def conv(h, carry_ref, cw_ref):
        h1, h2 = _shift_rows(h, carry_ref.at[f], tm)
        return cw_ref[0:1, :] * h2 + cw_ref[1:2, :] * h1 + cw_ref[2:3, :] * h

    hg = conv(_dot(xn, wg_ref[...]), carry_g_ref, cg_ref)
    hu = conv(_dot(xn, wu_ref[...]), carry_u_ref, cu_ref)
    act = hg * _sigmoid(hg) * hu
    acc_ref[...] += _dot(act.astype(BF16), wd_ref[...])

    @pl.when(f == nf - 1)
    def _():
        y = x_ref[0] + acc_ref[...]
        if final:
            ms = jnp.mean(y * y, axis=-1, keepdims=True)
            y = y * lax.rsqrt(ms + NORM_EPS) * fg_ref[...]
        o_ref[0] = y


def _ffn(x3d, g, w_up, conv_w, w_down, final_g, final, tm=512, fc=1408):
    b, t, _ = x3d.shape
    nf = D_FF // fc
    return pl.pallas_call(
        functools.partial(_ffn_kernel, tm=tm, final=final),
        grid=(b, t // tm, nf),
        in_specs=[pl.BlockSpec((1, tm, D_MODEL), lambda i, j, f: (i, j, 0)),
                  pl.BlockSpec((1, D_MODEL), lambda i, j, f: (0, 0)),
                  pl.BlockSpec((D_MODEL, fc), lambda i, j, f: (0, f)),
                  pl.BlockSpec((D_MODEL, fc), lambda i, j, f: (0, f + nf)),
                  pl.BlockSpec((8, fc), lambda i, j, f: (0, f)),
                  pl.BlockSpec((8, fc), lambda i, j, f: (0, f + nf)),
                  pl.BlockSpec((fc, D_MODEL), lambda i, j, f: (f, 0)),
                  pl.BlockSpec((1, D_MODEL), lambda i, j, f: (0, 0))],
        out_specs=pl.BlockSpec((1, tm, D_MODEL), lambda i, j, f: (i, j, 0)),
        out_shape=jax.ShapeDtypeStruct((b, t, D_MODEL), F32),
        scratch_shapes=[pltpu.VMEM((tm, D_MODEL), BF16),
                        pltpu.VMEM((tm, D_MODEL), F32),
                        pltpu.VMEM((nf, 8, fc), F32),
                        pltpu.VMEM((nf, 8, fc), F32)],
        compiler_params=pltpu.CompilerParams(
            dimension_semantics=("parallel", "arbitrary", "arbitrary"),
            vmem_limit_bytes=VMEM_LIMIT),
        name="conv_ffn",
    )(x3d, g, w_up, w_up, conv_w, conv_w, w_down, final_g)


def _pad_rows(a, rows=8):
    return jnp.pad(a, ((0, rows - a.shape[0]), (0, 0)))


def _pack_w_in(w):
    conv_cols = 3 * WIDTH
    rwkv_cols = 3 * WIDTH + DECAY_RANK + ICLR_RANK + GATE_RANK
    n_heads = WIDTH // HEAD_DIM
    o1 = conv_cols
    o2 = o1 + rwkv_cols
    o3 = o2 + AT_COLS
    o4 = o3 + n_heads
    w_gc = jnp.concatenate([w[:, o4:], w[:, :o1]], axis=1)
    pad = jnp.zeros((w.shape[0], RW_COLS - RW_USED - n_heads), w.dtype)
    w_rw = jnp.concatenate([w[:, o1:o2], w[:, o3:o4], pad], axis=1)
    return w_gc.astype(BF16), w_rw.astype(BF16), w[:, o2:o3].astype(BF16)


def kernel(x, norm1_g, w_in, gate_b, conv_mix_w, rwkv_mu, rwkv_w0, rwkv_w_up, rwkv_a0,
           rwkv_a_up, rwkv_g_up, rwkv_k_k, rwkv_k_a, rwkv_r_k, rwkv_gn_g, rwkv_gn_b,
           attn_forget_b, w_branch, w_o, norm2_g, ffn_w_up, ffn_conv_w, ffn_w_down,
           final_norm_g):
    b, t, d = x.shape
    depth = w_in.shape[0]
    n_heads = WIDTH // HEAD_DIM
    tf = 256
    tq = 512
    n_chunk = 4
    tt = n_chunk * CHUNK

    lane = np.arange(LANES)
    seg = jnp.asarray((lane[:, None] // HEAD_DIM) == (lane[None, :] // HEAD_DIM), BF16)
    pos = np.arange(tt)
    tri_c = jnp.asarray((pos[:, None] >= pos[None, :]) & (pos[:, None] // CHUNK == pos[None, :] // CHUNK),
                        BF16)
    prep_consts = _attn_prep_constants(tf)

    for l in range(depth):
        w_gc, w_rw, w_at = _pack_w_in(w_in[l])
        x2d = x.reshape(b * t, d)
        g1 = norm1_g[l][None, :]
        zgc = _inproj(x2d, g1, w_gc, tn=GC_COLS // 3).reshape(b, t, GC_COLS)
        zr = _inproj(x2d, g1, w_rw, tn=RW_COLS // 2).reshape(b, t, RW_COLS)
        za = _inproj(x2d, g1, w_at, tn=AT_COLS).reshape(b, t, AT_COLS)

        wwa = jnp.zeros((LANES, 2 * WIDTH), F32)
        wwa = wwa.at[:DECAY_RANK, :WIDTH].set(rwkv_w_up[l]).at[DECAY_RANK:, WIDTH:].set(rwkv_a_up[l])
        w0a0 = jnp.concatenate([rwkv_w0[l], rwkv_a0[l]])[None, :]
        vecs = _pad_rows(jnp.stack([rwkv_k_k[l], rwkv_k_a[l], rwkv_r_k[l].reshape(WIDTH),
                                    rwkv_gn_g[l], rwkv_gn_b[l]]))
        y_b = _rwkv(zr, rwkv_mu[l][None, :], wwa.astype(BF16), w0a0,
                    rwkv_g_up[l].astype(BF16), vecs, seg, tri_c, n_chunk)

        b_f = jnp.pad(attn_forget_b[l], (0, LANES - n_heads))[None, :]
        q_aug, k_aug, v_aug = _attn_prep(za, zr, b_f, prep_consts, tf)
        y_c = _attention(q_aug, k_aug, v_aug, tq)

        x = _merge(zgc, y_b, y_c, x, _pad_rows(conv_mix_w[l]), gate_b[l][None, :],
                   w_branch[l].astype(BF16), w_o[l].astype(BF16))
        x = _ffn(x, norm2_g[l][None, :], ffn_w_up[l].astype(BF16), _pad_rows(ffn_conv_w[l]),
                 ffn_w_down[l].astype(BF16), final_norm_g[None, :], final=(l == depth - 1))
    return x
```

```python
import functools
import math

import numpy as np
import jax
import jax.numpy as jnp
from jax import lax
from jax.experimental import pallas as pl
from jax.experimental.pallas import tpu as pltpu

F32 = jnp.float32
BF16 = jnp.bfloat16

D_MODEL = 1024
HEAD_DIM = 64
WIDTH = 512
N_PAIR = WIDTH // (2 * HEAD_DIM)
LANES = 128
DECAY_RANK = 64
ICLR_RANK = 64
GATE_RANK = 128
D_FF = 2816
NORM_EPS = 1e-6
GN_EPS = 64e-5
DECAY_SCALE = math.exp(-0.5)
CHUNK = 64
LOG2E = math.log2(math.e)
NEG_BIG = -1e30

GC_COLS = 3 * D_MODEL + 3 * WIDTH
RW_COLS = 2048
RW_USED = 1792
AT_COLS = 3 * WIDTH

VMEM_LIMIT = 56 * 1024 * 1024


def _dot(a, b):
    return jnp.dot(a, b, preferred_element_type=F32)


def _dot_nt(a, b):
    return lax.dot_general(a, b, (((1,), (1,)), ((), ())), preferred_element_type=F32)


def _bdot(a, b):
    return _dot(a.astype(BF16), b.astype(BF16))


def _bdot_nt(a, b):
    return _dot_nt(a.astype(BF16), b.astype(BF16))


def _split3(x):
    hi = x.astype(BF16)
    r1 = x - hi.astype(F32)
    mid = r1.astype(BF16)
    lo = (r1 - mid.astype(F32)).astype(BF16)
    return hi, mid, lo


def _dot_x01(x, m01):
    hi, mid, lo = _split3(x)
    return _dot(hi, m01) + _dot(mid, m01) + _dot(lo, m01)


def _dot_01x(m01, x):
    hi, mid, lo = _split3(x)
    return _dot(m01, hi) + _dot(m01, mid) + _dot(m01, lo)


def _sigmoid(x):
    return 1.0 / (1.0 + jnp.exp(-x))


def _inproj_kernel(x_ref, g_ref, w_ref, o_ref, xn_ref):
    @pl.when(pl.program_id(1) == 0)
    def _():
        x = x_ref[...]
        ms = jnp.mean(x * x, axis=-1, keepdims=True)
        xn_ref[...] = (x * lax.rsqrt(ms + NORM_EPS) * g_ref[...]).astype(BF16)

    o_ref[...] = _dot(xn_ref[...], w_ref[...]).astype(o_ref.dtype)


def _inproj(x2d, g, w, tn, out_dtype, tm=1024):
    m = x2d.shape[0]
    n = w.shape[1]
    return pl.pallas_call(
        _inproj_kernel,
        grid=(m // tm, n // tn),
        in_specs=[pl.BlockSpec((tm, D_MODEL), lambda i, j: (i, 0)),
                  pl.BlockSpec((1, D_MODEL), lambda i, j: (0, 0)),
                  pl.BlockSpec((D_MODEL, tn), lambda i, j: (0, j))],
        out_specs=pl.BlockSpec((tm, tn), lambda i, j: (i, j)),
        out_shape=jax.ShapeDtypeStruct((m, n), out_dtype),
        scratch_shapes=[pltpu.VMEM((tm, D_MODEL), BF16)],
        compiler_params=pltpu.CompilerParams(
            dimension_semantics=("parallel", "arbitrary"), vmem_limit_bytes=VMEM_LIMIT),
        name="inproj",
    )(x2d, g, w)


def _unit_lower_inverses(mats):
    n = mats[0].shape[0]
    r = lax.broadcasted_iota(jnp.int32, (n, n), 0)
    c = lax.broadcasted_iota(jnp.int32, (n, n), 1)
    same16 = (r >> 4) == (c >> 4)
    same32 = (r >> 5) == (c >> 5)
    eye = jnp.where(r == c, 1.0, 0.0)
    a16 = [jnp.where(same16, a, 0.0).astype(BF16) for a in mats]
    ts = [eye + a.astype(F32) for a in a16]
    ps = [_dot(a, a) for a in a16]
    for step in range(3):
        pb = [p.astype(BF16) for p in ps]
        ts = [t + _dot(t.astype(BF16), p) for t, p in zip(ts, pb)]
        if step < 2:
            ps = [_dot(p, p) for p in pb]
    for blocks in (lambda a: jnp.where(same32, jnp.where(same16, 0.0, a), 0.0),
                   lambda a: jnp.where(same32, 0.0, a)):
        tb = [t.astype(BF16) for t in ts]
        mid = [_dot(t, blocks(a).astype(BF16)).astype(BF16) for t, a in zip(tb, mats)]
        ts = [t + _dot(m, t_b) for t, m, t_b in zip(ts, mid, tb)]
    return ts


def _rwkv_kernel(z_ref, mu_ref, wwa_ref, w0a0_ref, gup_ref, vec_ref, seg_ref, tri_ref,
                 y_ref, state_ref, last_ref, *, n_chunk):
    c = CHUNK
    tt = n_chunk * c

    @pl.when(pl.program_id(1) == 0)
    def _():
        state_ref[...] = jnp.zeros_like(state_ref)
        last_ref[...] = jnp.zeros_like(last_ref)

    z = z_ref[0][:, :RW_USED]
    row = lax.broadcasted_iota(jnp.int32, (tt, 1), 0)
    prev = jnp.where(row == 0, last_ref[0:1, :], pltpu.roll(z, 1, axis=0))
    last_ref[0:1, :] = z[tt - 1:tt, :]
    zl = z + mu_ref[...] * (prev - z)

    lane = lax.broadcasted_iota(jnp.int32, (1, LANES), 1)
    first = lane < HEAD_DIM
    wa = zl[:, 3 * WIDTH:3 * WIDTH + LANES]
    wa = jnp.where(first, jnp.tanh(wa), wa)
    wa_out = _dot(wa.astype(BF16), wwa_ref[...]) + w0a0_ref[...]
    gd = zl[:, 3 * WIDTH + LANES:3 * WIDTH + 2 * LANES]
    g_all = _dot(_sigmoid(gd).astype(BF16), gup_ref[...])

    seg = seg_ref[...]
    tri = tri_ref[...]
    r2 = lax.broadcasted_iota(jnp.int32, (2 * c, 2 * c), 0)
    c2 = lax.broadcasted_iota(jnp.int32, (2 * c, 2 * c), 1)
    strict = ((r2 >> 6) == (c2 >> 6)) & ((c2 & (c - 1)) < (r2 & (c - 1)))
    r1 = lax.broadcasted_iota(jnp.int32, (c, 2 * c), 0)
    c1 = lax.broadcasted_iota(jnp.int32, (c, 2 * c), 1)
    incl = (c1 & (c - 1)) <= r1

    def stack(x):
        return jnp.concatenate([jnp.where(first, x, 0.0), jnp.where(first, 0.0, x)], axis=0)

    pairs = range(N_PAIR)
    chunks = range(n_chunk)
    lanes = [slice(p * LANES, (p + 1) * LANES) for p in pairs]
    r_ = [zl[:, sl] for sl in lanes]
    k_ = [zl[:, WIDTH + p * LANES:WIDTH + (p + 1) * LANES] for p in pairs]
    v_ = [zl[:, 2 * WIDTH + p * LANES:2 * WIDTH + (p + 1) * LANES] for p in pairs]
    logw = [-DECAY_SCALE * _sigmoid(wa_out[:, sl]) for sl in lanes]
    a_ = [_sigmoid(wa_out[:, WIDTH + p * LANES:WIDTH + (p + 1) * LANES]) for p in pairs]
    kap = [k_[p] * vec_ref[0:1, lanes[p]] for p in pairs]
    kss = [_dot_x01(kap[p] * kap[p], seg) for p in pairs]
    kh = [kap[p] * lax.rsqrt(kss[p] + 1e-12) for p in pairs]
    kt = [k_[p] * (1.0 + (a_[p] - 1.0) * vec_ref[1:2, lanes[p]]) for p in pairs]
    bt = [a_[p] * kh[p] for p in pairs]
    rks = [_dot_x01(r_[p] * kt[p] * vec_ref[2:3, lanes[p]], seg) for p in pairs]
    cum = [_dot_01x(tri, logw[p]) for p in pairs]
    e_neg = [jnp.exp(-cum[p]) for p in pairs]
    qa = [-kh[p] * jnp.exp(cum[p] - logw[p]) for p in pairs]
    qr = [(r_[p] * jnp.exp(cum[p])).astype(BF16) for p in pairs]
    kb = [bt[p] * e_neg[p] for p in pairs]
    kk = [kt[p] * e_neg[p] for p in pairs]

    pc = [(p, ch) for p in pairs for ch in chunks]
    rows = [slice(ch * c, (ch + 1) * c) for ch in chunks]
    cum_c = {(p, ch): cum[p][ch * c + c - 1:ch * c + c, :] for p, ch in pc}
    e_rem = {k: jnp.exp(cum_c[k] - cum[k[0]][rows[k[1]]]) for k in pc}
    qa_st = {(p, ch): stack(qa[p][rows[ch]]).astype(BF16) for p, ch in pc}
    v_st = {(p, ch): stack(v_[p][rows[ch]]) for p, ch in pc}
    v_sb = {k: v_st[k].astype(BF16) for k in pc}
    aa = {}
    for p, ch in pc:
        q3 = jnp.concatenate([qa_st[p, ch], qr[p][rows[ch]]], axis=0)
        k3 = jnp.concatenate([stack(kb[p][rows[ch]]), stack(kk[p][rows[ch]])], axis=0)
        aa[p, ch] = _dot_nt(q3, k3.astype(BF16))
    a_ab = [jnp.where(strict, aa[k][:2 * c, :2 * c], 0.0) for k in pc]
    a_ak = {k: jnp.where(strict, aa[k][:2 * c, 2 * c:], 0.0).astype(BF16) for k in pc}
    a_rb = {k: jnp.where(incl, aa[k][2 * c:, :2 * c], 0.0).astype(BF16) for k in pc}
    a_rk = {k: jnp.where(incl, aa[k][2 * c:, 2 * c:], 0.0).astype(BF16) for k in pc}

    t_inv = dict(zip(pc, _unit_lower_inverses(a_ab)))
    akv = {k: _dot(a_ak[k], v_sb[k]).astype(BF16) for k in pc}
    tw = {k: _dot(t_inv[k].astype(BF16), jnp.concatenate([akv[k], qa_st[k]], axis=1)) for k in pc}
    uv_t = {k: tw[k][:, :LANES].T for k in pc}
    wq_st = {k: tw[k][:, LANES:].astype(BF16) for k in pc}
    vkk = {(p, ch): _bdot(v_st[p, ch].T, stack(kt[p][rows[ch]] * e_rem[p, ch])) for p, ch in pc}
    kb2 = {(p, ch): stack(bt[p][rows[ch]] * e_rem[p, ch]).astype(BF16) for p, ch in pc}
    ark_v = {k: _dot(a_rk[k], v_sb[k]) for k in pc}
    decay_c = {k: jnp.exp(cum_c[k]) for k in pc}

    s = [state_ref[p] for p in pairs]
    ys = {}
    for ch in chunks:
        s_b = [s[p].astype(BF16) for p in pairs]
        u_b = [(_dot_nt(s_b[p], wq_st[p, ch]) + uv_t[p, ch]).astype(BF16) for p in pairs]
        s = [s[p] * decay_c[p, ch] + _dot(u_b[p], kb2[p, ch]) + vkk[p, ch] for p in pairs]
        for p in pairs:
            ys[p, ch] = (_dot_nt(qr[p][rows[ch]], s_b[p]) + _dot_nt(a_rb[p, ch], u_b[p])
                         + ark_v[p, ch])
    for p in pairs:
        state_ref[p] = s[p]

    for p in pairs:
        sl = lanes[p]
        y = jnp.concatenate([ys[p, ch] for ch in chunks], axis=0) if n_chunk > 1 else ys[p, 0]
        mean = _dot_x01(y, seg) * (1.0 / HEAD_DIM)
        d = y - mean
        var = _dot_x01(d * d, seg) * (1.0 / HEAD_DIM)
        yn = d * lax.rsqrt(var + GN_EPS) * vec_ref[3:4, sl] + vec_ref[4:5, sl]
        y_ref[0, :, sl] = (yn + rks[p] * v_[p]) * g_all[:, sl]


def _rwkv(zr, mu, wwa, w0a0, gup, vecs, seg, tri, n_chunk):
    b, t, _ = zr.shape
    tt = n_chunk * CHUNK
    const = lambda shape: pl.BlockSpec(shape, lambda i, j: (0,) * len(shape))
    return pl.pallas_call(
        functools.partial(_rwkv_kernel, n_chunk=n_chunk),
        grid=(b, t // tt),
        in_specs=[pl.BlockSpec((1, tt, RW_COLS), lambda i, j: (i, j, 0)),
                  const((1, RW_USED)), const((LANES, 2 * WIDTH)), const((1, 2 * WIDTH)),
                  const((GATE_RANK, WIDTH)), const((8, WIDTH)), const((LANES, LANES)),
                  const((tt, tt))],
        out_specs=pl.BlockSpec((1, tt, WIDTH), lambda i, j: (i, j, 0)),
        out_shape=jax.ShapeDtypeStruct((b, t, WIDTH), F32),
        scratch_shapes=[pltpu.VMEM((N_PAIR, LANES, LANES), F32),
                        pltpu.VMEM((8, RW_USED), F32)],
        compiler_params=pltpu.CompilerParams(
            dimension_semantics=("parallel", "arbitrary"), vmem_limit_bytes=VMEM_LIMIT),
        name="rwkv7",
    )(zr, mu, wwa, w0a0, gup, vecs, seg, tri)


def _attn_prep_kernel(za_ref, f_ref, b_ref, tri_ref, selq_ref, selk_ref, oneq_ref, onek_ref,
                      q_ref, k_ref, v_ref, carry_ref):
    @pl.when(pl.program_id(1) == 0)
    def _():
        carry_ref[...] = jnp.zeros_like(carry_ref)

    f = f_ref[0] + b_ref[...]
    lf = (jnp.minimum(f, 0.0) - jnp.log(1.0 + jnp.exp(-jnp.abs(f)))) * LOG2E
    cum = _dot_01x(tri_ref[...], lf) + carry_ref[0:1, :]
    n = cum.shape[0]
    carry_ref[0:1, :] = cum[n - 1:n, :]
    parts = jnp.concatenate(_split3(cum), axis=1)
    aq = _dot(parts, selq_ref[...]) + oneq_ref[...]
    ak = _dot(parts, selk_ref[...]) + onek_ref[...]

    lane = lax.broadcasted_iota(jnp.int32, (1, LANES), 1)
    first = lane < HEAD_DIM
    one_lane0 = jnp.where(lane == 0, 1.0, 0.0)
    za = za_ref[0]
    for p in range(N_PAIR):
        q = za[:, p * LANES:(p + 1) * LANES]
        zero = jnp.zeros_like(q)
        for e in range(2):
            i = 2 * p + e
            q_ref[0, p, e, :, :LANES] = jnp.where(first, q, zero) if e == 0 else jnp.where(first, zero, q)
            q_ref[0, p, e, :, LANES:] = aq[:, i * LANES:(i + 1) * LANES].astype(BF16)
        k_ref[0, p, :, :LANES] = za[:, WIDTH + p * LANES:WIDTH + (p + 1) * LANES]
        k_ref[0, p, :, LANES:] = ak[:, p * LANES:(p + 1) * LANES].astype(BF16)
        v_ref[0, p, :, :LANES] = za[:, 2 * WIDTH + p * LANES:2 * WIDTH + (p + 1) * LANES]
        v_ref[0, p, :, LANES:] = jnp.broadcast_to(one_lane0, (n, LANES)).astype(BF16)


def _attn_prep(za, zr, b_f, consts, tf):
    b, t, _ = za.shape
    tri, selq, selk, oneq, onek = consts
    const = lambda shape: pl.BlockSpec(shape, lambda i, j: (0,) * len(shape))
    return pl.pallas_call(
        _attn_prep_kernel,
        grid=(b, t // tf),
        in_specs=[pl.BlockSpec((1, tf, AT_COLS), lambda i, j: (i, j, GC_COLS // AT_COLS)),
                  pl.BlockSpec((1, tf, LANES), lambda i, j: (i, j, RW_USED // LANES)),
                  const((1, LANES)), const((tf, tf)), const((3 * LANES, 8 * LANES)),
                  const((3 * LANES, 4 * LANES)), const((1, 8 * LANES)), const((1, 4 * LANES))],
        out_specs=[pl.BlockSpec((1, N_PAIR, 2, tf, 2 * LANES), lambda i, j: (i, 0, 0, j, 0)),
                   pl.BlockSpec((1, N_PAIR, tf, 2 * LANES), lambda i, j: (i, 0, j, 0)),
                   pl.BlockSpec((1, N_PAIR, tf, 2 * LANES), lambda i, j: (i, 0, j, 0))],
        out_shape=[jax.ShapeDtypeStruct((b, N_PAIR, 2, t, 2 * LANES), BF16),
                   jax.ShapeDtypeStruct((b, N_PAIR, t, 2 * LANES), BF16),
                   jax.ShapeDtypeStruct((b, N_PAIR, t, 2 * LANES), BF16)],
        scratch_shapes=[pltpu.VMEM((8, LANES), F32)],
        compiler_params=pltpu.CompilerParams(
            dimension_semantics=("parallel", "arbitrary"), vmem_limit_bytes=VMEM_LIMIT),
        name="attn_prep",
    )(za, zr, b_f, tri, selq, selk, oneq, onek)


def _attn_prep_constants(tf):
    selq = np.zeros((3 * LANES, 8 * LANES), np.float32)
    selk = np.zeros((3 * LANES, N_PAIR * LANES), np.float32)
    oneq = np.zeros((1, 8 * LANES), np.float32)
    onek = np.zeros((1, N_PAIR * LANES), np.float32)
    for p in range(N_PAIR):
        for e in range(2):
            h = 2 * p + e
            for part in range(3):
                selq[part * LANES + h, h * LANES + 8 * e + part] = 1.0
                oneq[0, h * LANES + 8 * e + 3 + part] = 1.0
                selk[part * LANES + h, p * LANES + 8 * e + 3 + part] = -1.0
                onek[0, p * LANES + 8 * e + part] = 1.0
    tri = np.tril(np.ones((tf, tf), np.float32))
    return (jnp.asarray(tri, BF16), jnp.asarray(selq, BF16), jnp.asarray(selk, BF16),
            jnp.asarray(oneq), jnp.asarray(onek))


def _attn_kernel(q_ref, k_ref, v_ref, o_ref, m_ref, acc_ref, sa_ref, sb_ref, *, tq):
    i = pl.program_id(2)
    q = q_ref[0, 0].reshape(2 * tq, 2 * LANES)
    m_ref[...] = jnp.full_like(m_ref, NEG_BIG)
    acc_ref[...] = jnp.zeros_like(acc_ref)

    def scores(j, s_ref):
        start = pl.multiple_of(j * tq, tq)
        s_ref[...] = _dot_nt(q, k_ref[0, 0, pl.ds(start, tq), :])

    def block(j, s_ref, masked):
        start = pl.multiple_of(j * tq, tq)
        s = s_ref[...]
        if masked:
            row = lax.broadcasted_iota(jnp.int32, (2 * tq, tq), 0) & (tq - 1)
            col = lax.broadcasted_iota(jnp.int32, (2 * tq, tq), 1)
            s = jnp.where(col <= row, s, NEG_BIG)
        m_prev = m_ref[...]
        m_new = jnp.maximum(m_prev, jnp.max(s, axis=1, keepdims=True))
        p = jnp.exp2(s - m_new)
        acc_ref[...] = (jnp.exp2(m_prev - m_new) * acc_ref[...]
                        + _dot(p.astype(BF16), v_ref[0, 0, pl.ds(start, tq), :]))
        m_ref[...] = m_new

    scores(0, sa_ref)

    def body(jj, carry):
        scores(2 * jj + 1, sb_ref)
        block(2 * jj, sa_ref, False)
        scores(2 * jj + 2, sa_ref)
        block(2 * jj + 1, sb_ref, False)
        return carry

    lax.fori_loop(0, i // 2, body, 0)

    @pl.when(i % 2 == 0)
    def _():
        block(i, sa_ref, True)

    @pl.when(i % 2 == 1)
    def _():
        scores(i, sb_ref)
        block(i - 1, sa_ref, False)
        block(i, sb_ref, True)

    acc = acc_ref[...]
    o = acc[:, :LANES] / acc[:, LANES:LANES + 1]
    lane = lax.broadcasted_iota(jnp.int32, (1, LANES), 1)
    o_ref[0] = jnp.where(lane < HEAD_DIM, o[:tq], o[tq:])


def _attention(q_aug, k_aug, v_aug, tq):
    b, _, _, t, _ = q_aug.shape
    return pl.pallas_call(
        functools.partial(_attn_kernel, tq=tq),
        grid=(b, N_PAIR, t // tq),
        in_specs=[pl.BlockSpec((1, 1, 2, tq, 2 * LANES), lambda bi, p, i: (bi, p, 0, i, 0)),
                  pl.BlockSpec((1, 1, t, 2 * LANES), lambda bi, p, i: (bi, p, 0, 0)),
                  pl.BlockSpec((1, 1, t, 2 * LANES), lambda bi, p, i: (bi, p, 0, 0))],
        out_specs=pl.BlockSpec((1, tq, LANES), lambda bi, p, i: (bi, i, p)),
        out_shape=jax.ShapeDtypeStruct((b, t, WIDTH), F32),
        scratch_shapes=[pltpu.VMEM((2 * tq, 1), F32),
                        pltpu.VMEM((2 * tq, 2 * LANES), F32),
                        pltpu.VMEM((2 * tq, tq), F32),
                        pltpu.VMEM((2 * tq, tq), F32)],
        compiler_params=pltpu.CompilerParams(
            dimension_semantics=("parallel", "parallel", "arbitrary"),
            vmem_limit_bytes=VMEM_LIMIT),
        name="forget_attention",
    )(q_aug, k_aug, v_aug)


def _shift_rows(u, carry_ref, tm):
    row = lax.broadcasted_iota(jnp.int32, (tm, 1), 0)
    prev1 = carry_ref[1:2, :]
    prev2 = carry_ref[0:1, :]
    u1 = jnp.where(row == 0, prev1, pltpu.roll(u, 1, axis=0))
    u2 = jnp.where(row == 0, prev2, jnp.where(row == 1, prev1, pltpu.roll(u, 2, axis=0)))
    carry_ref[0:2, :] = u[tm - 2:tm, :]
    return u1, u2


def _merge_kernel(zg_ref, zc_ref, yb_ref, yc_ref, x_ref, cw_ref, gb_ref, wb_ref, wo_ref,
                  o_ref, carry_ref, *, tm):
    @pl.when(pl.program_id(1) == 0)
    def _():
        carry_ref[...] = jnp.zeros_like(carry_ref)

    zc = zc_ref[0].astype(F32)
    u = zc[:, WIDTH:2 * WIDTH] * zc[:, 2 * WIDTH:]
    u1, u2 = _shift_rows(u, carry_ref, tm)
    ya = zc[:, :WIDTH] * (cw_ref[0:1, :] * u2 + cw_ref[1:2, :] * u1 + cw_ref[2:3, :] * u)
    gates = _sigmoid(zg_ref[0].astype(F32) + gb_ref[...])
    merged = (gates[:, :D_MODEL] * _dot(ya.astype(BF16), wb_ref[0])
              + gates[:, D_MODEL:2 * D_MODEL] * _dot(yb_ref[0].astype(BF16), wb_ref[1])
              + gates[:, 2 * D_MODEL:] * _dot(yc_ref[0].astype(BF16), wb_ref[2]))
    o_ref[0] = x_ref[0] + _dot(merged.astype(BF16), wo_ref[...])


def _merge(zgc, yb, yc, x3d, conv_w, gate_b, w_branch, w_o, tm=512):
    b, t, _ = zgc.shape
    const = lambda shape: pl.BlockSpec(shape, lambda i, j: (0,) * len(shape))
    tile = lambda w, col: pl.BlockSpec((1, tm, w), lambda i, j: (i, j, col))
    return pl.pallas_call(
        functools.partial(_merge_kernel, tm=tm),
        grid=(b, t // tm),
        in_specs=[tile(3 * D_MODEL, 0), tile(3 * WIDTH, 3 * D_MODEL // (3 * WIDTH)),
                  tile(WIDTH, 0), tile(WIDTH, 0), tile(D_MODEL, 0),
                  const((8, WIDTH)), const((1, 3 * D_MODEL)),
                  const((3, WIDTH, D_MODEL)), const((D_MODEL, D_MODEL))],
        out_specs=tile(D_MODEL, 0),
        out_shape=jax.ShapeDtypeStruct((b, t, D_MODEL), F32),
        scratch_shapes=[pltpu.VMEM((8, WIDTH), F32)],
        compiler_params=pltpu.CompilerParams(
            dimension_semantics=("parallel", "arbitrary"), vmem_limit_bytes=VMEM_LIMIT),
        name="merge",
    )(zgc, zgc, yb, yc, x3d, conv_w, gate_b, w_branch, w_o)


def _ffn_kernel(x_ref, g_ref, wg_ref, wu_ref, cg_ref, cu_ref, wd_ref, fg_ref, o_ref,
                xn_ref, acc_ref, carry_g_ref, carry_u_ref, *, tm, final):
    f = pl.program_id(2)
    nf = pl.num_programs(2)

    @pl.when((pl.program_id(1) == 0) & (f == 0))
    def _():
        carry_g_ref[...] = jnp.zeros_like(carry_g_ref)
        carry_u_ref[...] = jnp.zeros_like(carry_u_ref)

    @pl.when(f == 0)
    def _():
        x = x_ref[0]
        ms = jnp.mean(x * x, axis=-1, keepdims=True)
        xn_ref[...] = (x * lax.rsqrt(ms + NORM_EPS) * g_ref[...]).astype(BF16)
        acc_ref[...] = jnp.zeros_like(acc_ref)

    xn = xn_ref[...]

    def conv(h, carry_ref, cw_ref):
        h1, h2 = _shift_rows(h, carry_ref.at[f], tm)
        return cw_ref[0:1, :] * h2 + cw_ref[1:2, :] * h1 + cw_ref[2:3, :] * h

    hg = conv(_dot(xn, wg_ref[...]), carry_g_ref, cg_ref)
    hu = conv(_dot(xn, wu_ref[...]), carry_u_ref, cu_ref)
    act = hg * _sigmoid(hg) * hu
    acc_ref[...] += _dot(act.astype(BF16), wd_ref[...])

    @pl.when(f == nf - 1)
    def _():
        y = x_ref[0] + acc_ref[...]
        if final:
            ms = jnp.mean(y * y, axis=-1, keepdims=True)
            y = y * lax.rsqrt(ms + NORM_EPS) * fg_ref[...]
        o_ref[0] = y


def _ffn(x3d, g, w_up, conv_w, w_down, final_g, final, tm=512, fc=1408):
    b, t, _ = x3d.shape
    nf = D_FF // fc
    return pl.pallas_call(
        functools.partial(_ffn_kernel, tm=tm, final=final),
        grid=(b, t // tm, nf),
        in_specs=[pl.BlockSpec((1, tm, D_MODEL), lambda i, j, f: (i, j, 0)),
                  pl.BlockSpec((1, D_MODEL), lambda i, j, f: (0, 0)),
                  pl.BlockSpec((D_MODEL, fc), lambda i, j, f: (0, f)),
                  pl.BlockSpec((D_MODEL, fc), lambda i, j, f: (0, f + nf)),
                  pl.BlockSpec((8, fc), lambda i, j, f: (0, f)),
                  pl.BlockSpec((8, fc), lambda i, j, f: (0, f + nf)),
                  pl.BlockSpec((fc, D_MODEL), lambda i, j, f: (f, 0)),
                  pl.BlockSpec((1, D_MODEL), lambda i, j, f: (0, 0))],
        out_specs=pl.BlockSpec((1, tm, D_MODEL), lambda i, j, f: (i, j, 0)),
        out_shape=jax.ShapeDtypeStruct((b, t, D_MODEL), F32),
        scratch_shapes=[pltpu.VMEM((tm, D_MODEL), BF16),
                        pltpu.VMEM((tm, D_MODEL), F32),
                        pltpu.VMEM((nf, 8, fc), F32),
                        pltpu.VMEM((nf, 8, fc), F32)],
        compiler_params=pltpu.CompilerParams(
            dimension_semantics=("parallel", "arbitrary", "arbitrary"),
            vmem_limit_bytes=VMEM_LIMIT),
        name="conv_ffn",
    )(x3d, g, w_up, w_up, conv_w, conv_w, w_down, final_g)


def _pad_rows(a, rows=8):
    return jnp.pad(a, ((0, rows - a.shape[0]), (0, 0)))


def _pack_w_in(w):
    conv_cols = 3 * WIDTH
    rwkv_cols = 3 * WIDTH + DECAY_RANK + ICLR_RANK + GATE_RANK
    n_heads = WIDTH // HEAD_DIM
    o1 = conv_cols
    o2 = o1 + rwkv_cols
    o3 = o2 + AT_COLS
    o4 = o3 + n_heads
    w_q = w[:, o2:o2 + WIDTH] * (HEAD_DIM ** -0.5 * LOG2E)
    w_ga = jnp.concatenate([w[:, o4:], w[:, :o1], w_q, w[:, o2 + WIDTH:o3]], axis=1)
    pad = jnp.zeros((w.shape[0], RW_COLS - RW_USED - n_heads), w.dtype)
    w_rw = jnp.concatenate([w[:, o1:o2], w[:, o3:o4], pad], axis=1)
    return w_ga.astype(BF16), w_rw.astype(BF16)


def kernel(x, norm1_g, w_in, gate_b, conv_mix_w, rwkv_mu, rwkv_w0, rwkv_w_up, rwkv_a0,
           rwkv_a_up, rwkv_g_up, rwkv_k_k, rwkv_k_a, rwkv_r_k, rwkv_gn_g, rwkv_gn_b,
           attn_forget_b, w_branch, w_o, norm2_g, ffn_w_up, ffn_conv_w, ffn_w_down,
           final_norm_g):
    b, t, d = x.shape
    depth = w_in.shape[0]
    n_heads = WIDTH // HEAD_DIM
    tf = 256
    tq = 512
    n_chunk = 4
    tt = n_chunk * CHUNK

    lane = np.arange(LANES)
    seg = jnp.asarray((lane[:, None] // HEAD_DIM) == (lane[None, :] // HEAD_DIM), BF16)
    pos = np.arange(tt)
    tri_c = jnp.asarray((pos[:, None] >= pos[None, :]) & (pos[:, None] // CHUNK == pos[None, :] // CHUNK),
                        BF16)
    prep_consts = _attn_prep_constants(tf)

    for l in range(depth):
        w_ga, w_rw = _pack_w_in(w_in[l])
        x2d = x.reshape(b * t, d)
        g1 = norm1_g[l][None, :]
        zga = _inproj(x2d, g1, w_ga, AT_COLS, BF16).reshape(b, t, GC_COLS + AT_COLS)
        zr = _inproj(x2d, g1, w_rw, RW_COLS // 2, F32).reshape(b, t, RW_COLS)

        wwa = jnp.zeros((LANES, 2 * WIDTH), F32)
        wwa = wwa.at[:DECAY_RANK, :WIDTH].set(rwkv_w_up[l]).at[DECAY_RANK:, WIDTH:].set(rwkv_a_up[l])
        w0a0 = jnp.concatenate([rwkv_w0[l], rwkv_a0[l]])[None, :]
        vecs = _pad_rows(jnp.stack([rwkv_k_k[l], rwkv_k_a[l], rwkv_r_k[l].reshape(WIDTH),
                                    rwkv_gn_g[l], rwkv_gn_b[l]]))
        y_b = _rwkv(zr, rwkv_mu[l][None, :], wwa.astype(BF16), w0a0,
                    rwkv_g_up[l].astype(BF16), vecs, seg, tri_c, n_chunk)

        b_f = jnp.pad(attn_forget_b[l], (0, LANES - n_heads))[None, :]
        q_aug, k_aug, v_aug = _attn_prep(zga, zr, b_f, prep_consts, tf)
        y_c = _attention(q_aug, k_aug, v_aug, tq)

        x = _merge(zga, y_b, y_c, x, _pad_rows(conv_mix_w[l]), gate_b[l][None, :],
                   w_branch[l].astype(BF16), w_o[l].astype(BF16))
        x = _ffn(x, norm2_g[l][None, :], ffn_w_up[l].astype(BF16), _pad_rows(ffn_conv_w[l]),
                 ffn_w_down[l].astype(BF16), final_norm_g[None, :], final=(l == depth - 1))
    return x
```

```python
import functools
import math

import numpy as np
import jax
import jax.numpy as jnp
from jax import lax
from jax.experimental import pallas as pl
from jax.experimental.pallas import tpu as pltpu

F32 = jnp.float32
BF16 = jnp.bfloat16

D_MODEL = 1024
HEAD_DIM = 64
WIDTH = 512
N_PAIR = WIDTH // (2 * HEAD_DIM)
LANES = 128
DECAY_RANK = 64
ICLR_RANK = 64
GATE_RANK = 128
D_FF = 2816
NORM_EPS = 1e-6
GN_EPS = 64e-5
DECAY_SCALE = math.exp(-0.5)
CHUNK = 64
LOG2E = math.log2(math.e)
NEG_BIG = -1e30

GC_COLS = 3 * D_MODEL + 3 * WIDTH
RW_COLS = 2048
RW_USED = 1792
AT_COLS = 3 * WIDTH

VMEM_LIMIT = 56 * 1024 * 1024


def _dot(a, b):
    return jnp.dot(a, b, preferred_element_type=F32)


def _dot_nt(a, b):
    return lax.dot_general(a, b, (((1,), (1,)), ((), ())), preferred_element_type=F32)


def _bdot(a, b):
    return _dot(a.astype(BF16), b.astype(BF16))


def _bdot_nt(a, b):
    return _dot_nt(a.astype(BF16), b.astype(BF16))


def _split3(x):
    hi = x.astype(BF16)
    r1 = x - hi.astype(F32)
    mid = r1.astype(BF16)
    lo = (r1 - mid.astype(F32)).astype(BF16)
    return hi, mid, lo


def _dot_x01(x, m01):
    hi, mid, lo = _split3(x)
    return _dot(hi, m01) + _dot(mid, m01) + _dot(lo, m01)


def _dot_01x(m01, x):
    hi, mid, lo = _split3(x)
    return _dot(m01, hi) + _dot(m01, mid) + _dot(m01, lo)


def _sigmoid(x):
    return 1.0 / (1.0 + jnp.exp(-x))


def _inproj_kernel(x_ref, g_ref, w_ref, o_ref, xn_ref):
    @pl.when(pl.program_id(1) == 0)
    def _():
        x = x_ref[...]
        ms = jnp.mean(x * x, axis=-1, keepdims=True)
        xn_ref[...] = (x * lax.rsqrt(ms + NORM_EPS) * g_ref[...]).astype(BF16)

    o_ref[...] = _dot(xn_ref[...], w_ref[...]).astype(o_ref.dtype)


def _inproj(x2d, g, w, tn, out_dtype, tm=1024):
    m = x2d.shape[0]
    n = w.shape[1]
    return pl.pallas_call(
        _inproj_kernel,
        grid=(m // tm, n // tn),
        in_specs=[pl.BlockSpec((tm, D_MODEL), lambda i, j: (i, 0)),
                  pl.BlockSpec((1, D_MODEL), lambda i, j: (0, 0)),
                  pl.BlockSpec((D_MODEL, tn), lambda i, j: (0, j))],
        out_specs=pl.BlockSpec((tm, tn), lambda i, j: (i, j)),
        out_shape=jax.ShapeDtypeStruct((m, n), out_dtype),
        scratch_shapes=[pltpu.VMEM((tm, D_MODEL), BF16)],
        compiler_params=pltpu.CompilerParams(
            dimension_semantics=("parallel", "arbitrary"), vmem_limit_bytes=VMEM_LIMIT),
        name="inproj",
    )(x2d, g, w)


def _unit_lower_inverses(mats):
    n = mats[0].shape[0]
    r = lax.broadcasted_iota(jnp.int32, (n, n), 0)
    c = lax.broadcasted_iota(jnp.int32, (n, n), 1)
    same16 = (r >> 4) == (c >> 4)
    same32 = (r >> 5) == (c >> 5)
    eye = jnp.where(r == c, 1.0, 0.0)
    a16 = [jnp.where(same16, a, 0.0).astype(BF16) for a in mats]
    ts = [eye + a.astype(F32) for a in a16]
    ps = [_dot(a, a) for a in a16]
    for step in range(3):
        pb = [p.astype(BF16) for p in ps]
        ts = [t + _dot(t.astype(BF16), p) for t, p in zip(ts, pb)]
        if step < 2:
            ps = [_dot(p, p) for p in pb]
    for blocks in (lambda a: jnp.where(same32, jnp.where(same16, 0.0, a), 0.0),
                   lambda a: jnp.where(same32, 0.0, a)):
        tb = [t.astype(BF16) for t in ts]
        mid = [_dot(t, blocks(a).astype(BF16)).astype(BF16) for t, a in zip(tb, mats)]
        ts = [t + _dot(m, t_b) for t, m, t_b in zip(ts, mid, tb)]
    return ts


def _rwkv_kernel(z_ref, mu_ref, wwa_ref, w0a0_ref, gup_ref, vec_ref, seg_ref, tri_ref,
                 y_ref, state_ref, last_ref, *, n_chunk):
    c = CHUNK
    tt = n_chunk * c

    @pl.when(pl.program_id(1) == 0)
    def _():
        state_ref[...] = jnp.zeros_like(state_ref)
        last_ref[...] = jnp.zeros_like(last_ref)

    z = z_ref[0][:, :RW_USED]
    row = lax.broadcasted_iota(jnp.int32, (tt, 1), 0)
    prev = jnp.where(row == 0, last_ref[0:1, :], pltpu.roll(z, 1, axis=0))
    last_ref[0:1, :] = z[tt - 1:tt, :]
    zl = z + mu_ref[...] * (prev - z)

    lane = lax.broadcasted_iota(jnp.int32, (1, LANES), 1)
    first = lane < HEAD_DIM
    wa = zl[:, 3 * WIDTH:3 * WIDTH + LANES]
    wa = jnp.where(first, jnp.tanh(wa), wa)
    wa_out = _dot(wa.astype(BF16), wwa_ref[...]) + w0a0_ref[...]
    gd = zl[:, 3 * WIDTH + LANES:3 * WIDTH + 2 * LANES]
    g_all = _dot(_sigmoid(gd).astype(BF16), gup_ref[...])

    seg = seg_ref[...]
    tri = tri_ref[...]
    r2 = lax.broadcasted_iota(jnp.int32, (2 * c, 2 * c), 0)
    c2 = lax.broadcasted_iota(jnp.int32, (2 * c, 2 * c), 1)
    strict = ((r2 >> 6) == (c2 >> 6)) & ((c2 & (c - 1)) < (r2 & (c - 1)))
    r1 = lax.broadcasted_iota(jnp.int32, (c, 2 * c), 0)
    c1 = lax.broadcasted_iota(jnp.int32, (c, 2 * c), 1)
    incl = (c1 & (c - 1)) <= r1

    def stack(x):
        return jnp.concatenate([jnp.where(first, x, 0.0), jnp.where(first, 0.0, x)], axis=0)

    pairs = range(N_PAIR)
    chunks = range(n_chunk)
    lanes = [slice(p * LANES, (p + 1) * LANES) for p in pairs]
    r_ = [zl[:, sl] for sl in lanes]
    k_ = [zl[:, WIDTH + p * LANES:WIDTH + (p + 1) * LANES] for p in pairs]
    v_ = [zl[:, 2 * WIDTH + p * LANES:2 * WIDTH + (p + 1) * LANES] for p in pairs]
    logw = [-DECAY_SCALE * _sigmoid(wa_out[:, sl]) for sl in lanes]
    a_ = [_sigmoid(wa_out[:, WIDTH + p * LANES:WIDTH + (p + 1) * LANES]) for p in pairs]
    kap = [k_[p] * vec_ref[0:1, lanes[p]] for p in pairs]
    kss = [_dot_x01(kap[p] * kap[p], seg) for p in pairs]
    kh = [kap[p] * lax.rsqrt(kss[p] + 1e-12) for p in pairs]
    kt = [k_[p] * (1.0 + (a_[p] - 1.0) * vec_ref[1:2, lanes[p]]) for p in pairs]
    bt = [a_[p] * kh[p] for p in pairs]
    rks = [_dot_x01(r_[p] * kt[p] * vec_ref[2:3, lanes[p]], seg) for p in pairs]
    cum = [_dot_01x(tri, logw[p]) for p in pairs]
    e_neg = [jnp.exp(-cum[p]) for p in pairs]
    qa = [-kh[p] * jnp.exp(cum[p] - logw[p]) for p in pairs]
    qr = [(r_[p] * jnp.exp(cum[p])).astype(BF16) for p in pairs]
    kb = [bt[p] * e_neg[p] for p in pairs]
    kk = [kt[p] * e_neg[p] for p in pairs]

    pc = [(p, ch) for p in pairs for ch in chunks]
    rows = [slice(ch * c, (ch + 1) * c) for ch in chunks]
    cum_c = {(p, ch): cum[p][ch * c + c - 1:ch * c + c, :] for p, ch in pc}
    e_rem = {k: jnp.exp(cum_c[k] - cum[k[0]][rows[k[1]]]) for k in pc}
    qa_st = {(p, ch): stack(qa[p][rows[ch]]).astype(BF16) for p, ch in pc}
    v_st = {(p, ch): stack(v_[p][rows[ch]]) for p, ch in pc}
    v_sb = {k: v_st[k].astype(BF16) for k in pc}
    aa = {}
    for p, ch in pc:
        q3 = jnp.concatenate([qa_st[p, ch], qr[p][rows[ch]]], axis=0)
        k3 = jnp.concatenate([stack(kb[p][rows[ch]]), stack(kk[p][rows[ch]])], axis=0)
        aa[p, ch] = _dot_nt(q3, k3.astype(BF16))
    a_ab = [jnp.where(strict, aa[k][:2 * c, :2 * c], 0.0) for k in pc]
    a_ak = {k: jnp.where(strict, aa[k][:2 * c, 2 * c:], 0.0).astype(BF16) for k in pc}
    a_rb = {k: jnp.where(incl, aa[k][2 * c:, :2 * c], 0.0).astype(BF16) for k in pc}
    a_rk = {k: jnp.where(incl, aa[k][2 * c:, 2 * c:], 0.0).astype(BF16) for k in pc}

    t_inv = dict(zip(pc, _unit_lower_inverses(a_ab)))
    akv = {k: _dot(a_ak[k], v_sb[k]).astype(BF16) for k in pc}
    tw = {k: _dot(t_inv[k].astype(BF16), jnp.concatenate([akv[k], qa_st[k]], axis=1)) for k in pc}
    uv_t = {k: tw[k][:, :LANES].T for k in pc}
    wq_st = {k: tw[k][:, LANES:].astype(BF16) for k in pc}
    vkk = {(p, ch): _bdot(v_st[p, ch].T, stack(kt[p][rows[ch]] * e_rem[p, ch])) for p, ch in pc}
    kb2 = {(p, ch): stack(bt[p][rows[ch]] * e_rem[p, ch]).astype(BF16) for p, ch in pc}
    ark_v = {k: _dot(a_rk[k], v_sb[k]) for k in pc}
    decay_c = {k: jnp.exp(cum_c[k]) for k in pc}

    s = [state_ref[p] for p in pairs]
    ys = {}
    for ch in chunks:
        s_b = [s[p].astype(BF16) for p in pairs]
        u_b = [(_dot_nt(s_b[p], wq_st[p, ch]) + uv_t[p, ch]).astype(BF16) for p in pairs]
        s = [s[p] * decay_c[p, ch] + _dot(u_b[p], kb2[p, ch]) + vkk[p, ch] for p in pairs]
        for p in pairs:
            ys[p, ch] = (_dot_nt(qr[p][rows[ch]], s_b[p]) + _dot_nt(a_rb[p, ch], u_b[p])
                         + ark_v[p, ch])
    for p in pairs:
        state_ref[p] = s[p]

    for p in pairs:
        sl = lanes[p]
        y = jnp.concatenate([ys[p, ch] for ch in chunks], axis=0) if n_chunk > 1 else ys[p, 0]
        mean = _dot_x01(y, seg) * (1.0 / HEAD_DIM)
        d = y - mean
        var = _dot_x01(d * d, seg) * (1.0 / HEAD_DIM)
        yn = d * lax.rsqrt(var + GN_EPS) * vec_ref[3:4, sl] + vec_ref[4:5, sl]
        y_ref[0, :, sl] = ((yn + rks[p] * v_[p]) * g_all[:, sl]).astype(y_ref.dtype)


def _rwkv(zr, mu, wwa, w0a0, gup, vecs, seg, tri, n_chunk):
    b, t, _ = zr.shape
    tt = n_chunk * CHUNK
    const = lambda shape: pl.BlockSpec(shape, lambda i, j: (0,) * len(shape))
    return pl.pallas_call(
        functools.partial(_rwkv_kernel, n_chunk=n_chunk),
        grid=(b, t // tt),
        in_specs=[pl.BlockSpec((1, tt, RW_COLS), lambda i, j: (i, j, 0)),
                  const((1, RW_USED)), const((LANES, 2 * WIDTH)), const((1, 2 * WIDTH)),
                  const((GATE_RANK, WIDTH)), const((8, WIDTH)), const((LANES, LANES)),
                  const((tt, tt))],
        out_specs=pl.BlockSpec((1, tt, WIDTH), lambda i, j: (i, j, 0)),
        out_shape=jax.ShapeDtypeStruct((b, t, WIDTH), BF16),
        scratch_shapes=[pltpu.VMEM((N_PAIR, LANES, LANES), F32),
                        pltpu.VMEM((8, RW_USED), F32)],
        compiler_params=pltpu.CompilerParams(
            dimension_semantics=("parallel", "arbitrary"), vmem_limit_bytes=VMEM_LIMIT),
        name="rwkv7",
    )(zr, mu, wwa, w0a0, gup, vecs, seg, tri)


def _attn_prep_kernel(za_ref, f_ref, b_ref, tri_ref, selq_ref, selk_ref, oneq_ref, onek_ref,
                      q_ref, k_ref, v_ref, carry_ref):
    @pl.when(pl.program_id(1) == 0)
    def _():
        carry_ref[...] = jnp.zeros_like(carry_ref)

    f = f_ref[0] + b_ref[...]
    lf = (jnp.minimum(f, 0.0) - jnp.log(1.0 + jnp.exp(-jnp.abs(f)))) * LOG2E
    cum = _dot_01x(tri_ref[...], lf) + carry_ref[0:1, :]
    n = cum.shape[0]
    carry_ref[0:1, :] = cum[n - 1:n, :]
    parts = jnp.concatenate(_split3(cum), axis=1)
    aq = _dot(parts, selq_ref[...]) + oneq_ref[...]
    ak = _dot(parts, selk_ref[...]) + onek_ref[...]

    lane = lax.broadcasted_iota(jnp.int32, (1, LANES), 1)
    first = lane < HEAD_DIM
    one_lane0 = jnp.where(lane == 0, 1.0, 0.0)
    za = za_ref[0]
    for p in range(N_PAIR):
        q = za[:, p * LANES:(p + 1) * LANES]
        zero = jnp.zeros_like(q)
        for e in range(2):
            i = 2 * p + e
            q_ref[0, p, e, :, :LANES] = jnp.where(first, q, zero) if e == 0 else jnp.where(first, zero, q)
            q_ref[0, p, e, :, LANES:] = aq[:, i * LANES:(i + 1) * LANES].astype(BF16)
        k_ref[0, p, :, :LANES] = za[:, WIDTH + p * LANES:WIDTH + (p + 1) * LANES]
        k_ref[0, p, :, LANES:] = ak[:, p * LANES:(p + 1) * LANES].astype(BF16)
        v_ref[0, p, :, :LANES] = za[:, 2 * WIDTH + p * LANES:2 * WIDTH + (p + 1) * LANES]
        v_ref[0, p, :, LANES:] = jnp.broadcast_to(one_lane0, (n, LANES)).astype(BF16)


def _attn_prep(za, zr, b_f, consts, tf):
    b, t, _ = za.shape
    tri, selq, selk, oneq, onek = consts
    const = lambda shape: pl.BlockSpec(shape, lambda i, j: (0,) * len(shape))
    return pl.pallas_call(
        _attn_prep_kernel,
        grid=(b, t // tf),
        in_specs=[pl.BlockSpec((1, tf, AT_COLS), lambda i, j: (i, j, GC_COLS // AT_COLS)),
                  pl.BlockSpec((1, tf, LANES), lambda i, j: (i, j, RW_USED // LANES)),
                  const((1, LANES)), const((tf, tf)), const((3 * LANES, 8 * LANES)),
                  const((3 * LANES, 4 * LANES)), const((1, 8 * LANES)), const((1, 4 * LANES))],
        out_specs=[pl.BlockSpec((1, N_PAIR, 2, tf, 2 * LANES), lambda i, j: (i, 0, 0, j, 0)),
                   pl.BlockSpec((1, N_PAIR, tf, 2 * LANES), lambda i, j: (i, 0, j, 0)),
                   pl.BlockSpec((1, N_PAIR, tf, 2 * LANES), lambda i, j: (i, 0, j, 0))],
        out_shape=[jax.ShapeDtypeStruct((b, N_PAIR, 2, t, 2 * LANES), BF16),
                   jax.ShapeDtypeStruct((b, N_PAIR, t, 2 * LANES), BF16),
                   jax.ShapeDtypeStruct((b, N_PAIR, t, 2 * LANES), BF16)],
        scratch_shapes=[pltpu.VMEM((8, LANES), F32)],
        compiler_params=pltpu.CompilerParams(
            dimension_semantics=("parallel", "arbitrary"), vmem_limit_bytes=VMEM_LIMIT),
        name="attn_prep",
    )(za, zr, b_f, tri, selq, selk, oneq, onek)


def _attn_prep_constants(tf):
    selq = np.zeros((3 * LANES, 8 * LANES), np.float32)
    selk = np.zeros((3 * LANES, N_PAIR * LANES), np.float32)
    oneq = np.zeros((1, 8 * LANES), np.float32)
    onek = np.zeros((1, N_PAIR * LANES), np.float32)
    for p in range(N_PAIR):
        for e in range(2):
            h = 2 * p + e
            for part in range(3):
                selq[part * LANES + h, h * LANES + 8 * e + part] = 1.0
                oneq[0, h * LANES + 8 * e + 3 + part] = 1.0
                selk[part * LANES + h, p * LANES + 8 * e + 3 + part] = -1.0
                onek[0, p * LANES + 8 * e + part] = 1.0
    tri = np.tril(np.ones((tf, tf), np.float32))
    return (jnp.asarray(tri, BF16), jnp.asarray(selq, BF16), jnp.asarray(selk, BF16),
            jnp.asarray(oneq), jnp.asarray(onek))


def _attn_kernel(q_ref, k_ref, v_ref, o_ref, m_ref, acc_ref, sa_ref, sb_ref, *, tq):
    i = pl.program_id(2)
    q = q_ref[0, 0].reshape(2 * tq, 2 * LANES)
    m_ref[...] = jnp.full_like(m_ref, NEG_BIG)
    acc_ref[...] = jnp.zeros_like(acc_ref)

    def scores(j, s_ref):
        start = pl.multiple_of(j * tq, tq)
        s_ref[...] = _dot_nt(q, k_ref[0, 0, pl.ds(start, tq), :])

    def block(j, s_ref, masked):
        start = pl.multiple_of(j * tq, tq)
        s = s_ref[...]
        if masked:
            row = lax.broadcasted_iota(jnp.int32, (2 * tq, tq), 0) & (tq - 1)
            col = lax.broadcasted_iota(jnp.int32, (2 * tq, tq), 1)
            s = jnp.where(col <= row, s, NEG_BIG)
        m_prev = m_ref[...]
        m_new = jnp.maximum(m_prev, jnp.max(s, axis=1, keepdims=True))
        p = jnp.exp2(s - m_new)
        acc_ref[...] = (jnp.exp2(m_prev - m_new) * acc_ref[...]
                        + _dot(p.astype(BF16), v_ref[0, 0, pl.ds(start, tq), :]))
        m_ref[...] = m_new

    scores(0, sa_ref)

    def body(jj, carry):
        scores(2 * jj + 1, sb_ref)
        block(2 * jj, sa_ref, False)
        scores(2 * jj + 2, sa_ref)
        block(2 * jj + 1, sb_ref, False)
        return carry

    lax.fori_loop(0, i // 2, body, 0)

    @pl.when(i % 2 == 0)
    def _():
        block(i, sa_ref, True)

    @pl.when(i % 2 == 1)
    def _():
        scores(i, sb_ref)
        block(i - 1, sa_ref, False)
        block(i, sb_ref, True)

    acc = acc_ref[...]
    o = acc[:, :LANES] / acc[:, LANES:LANES + 1]
    lane = lax.broadcasted_iota(jnp.int32, (1, LANES), 1)
    o_ref[0] = jnp.where(lane < HEAD_DIM, o[:tq], o[tq:]).astype(o_ref.dtype)


def _attention(q_aug, k_aug, v_aug, tq):
    b, _, _, t, _ = q_aug.shape
    return pl.pallas_call(
        functools.partial(_attn_kernel, tq=tq),
        grid=(b, N_PAIR, t // tq),
        in_specs=[pl.BlockSpec((1, 1, 2, tq, 2 * LANES), lambda bi, p, i: (bi, p, 0, i, 0)),
                  pl.BlockSpec((1, 1, t, 2 * LANES), lambda bi, p, i: (bi, p, 0, 0)),
                  pl.BlockSpec((1, 1, t, 2 * LANES), lambda bi, p, i: (bi, p, 0, 0))],
        out_specs=pl.BlockSpec((1, tq, LANES), lambda bi, p, i: (bi, i, p)),
        out_shape=jax.ShapeDtypeStruct((b, t, WIDTH), BF16),
        scratch_shapes=[pltpu.VMEM((2 * tq, 1), F32),
                        pltpu.VMEM((2 * tq, 2 * LANES), F32),
                        pltpu.VMEM((2 * tq, tq), F32),
                        pltpu.VMEM((2 * tq, tq), F32)],
        compiler_params=pltpu.CompilerParams(
            dimension_semantics=("parallel", "parallel", "arbitrary"),
            vmem_limit_bytes=VMEM_LIMIT),
        name="forget_attention",
    )(q_aug, k_aug, v_aug)


def _shift_rows(u, carry_ref, tm):
    row = lax.broadcasted_iota(jnp.int32, (tm, 1), 0)
    prev1 = carry_ref[1:2, :]
    prev2 = carry_ref[0:1, :]
    u1 = jnp.where(row == 0, prev1, pltpu.roll(u, 1, axis=0))
    u2 = jnp.where(row == 0, prev2, jnp.where(row == 1, prev1, pltpu.roll(u, 2, axis=0)))
    carry_ref[0:2, :] = u[tm - 2:tm, :]
    return u1, u2


def _merge_kernel(zg_ref, zc_ref, yb_ref, yc_ref, x_ref, cw_ref, gb_ref, wb_ref, wo_ref,
                  o_ref, carry_ref, *, tm):
    @pl.when(pl.program_id(1) == 0)
    def _():
        carry_ref[...] = jnp.zeros_like(carry_ref)

    zc = zc_ref[0].astype(F32)
    u = zc[:, WIDTH:2 * WIDTH] * zc[:, 2 * WIDTH:]
    u1, u2 = _shift_rows(u, carry_ref, tm)
    ya = zc[:, :WIDTH] * (cw_ref[0:1, :] * u2 + cw_ref[1:2, :] * u1 + cw_ref[2:3, :] * u)
    gates = _sigmoid(zg_ref[0].astype(F32) + gb_ref[...])
    merged = (gates[:, :D_MODEL] * _dot(ya.astype(BF16), wb_ref[0])
              + gates[:, D_MODEL:2 * D_MODEL] * _dot(yb_ref[0], wb_ref[1])
              + gates[:, 2 * D_MODEL:] * _dot(yc_ref[0], wb_ref[2]))
    o_ref[0] = x_ref[0] + _dot(merged.astype(BF16), wo_ref[...])


def _merge(zgc, yb, yc, x3d, conv_w, gate_b, w_branch, w_o, tm=512):
    b, t, _ = zgc.shape
    const = lambda shape: pl.BlockSpec(shape, lambda i, j: (0,) * len(shape))
    tile = lambda w, col: pl.BlockSpec((1, tm, w), lambda i, j: (i, j, col))
    return pl.pallas_call(
        functools.partial(_merge_kernel, tm=tm),
        grid=(b, t // tm),
        in_specs=[tile(3 * D_MODEL, 0), tile(3 * WIDTH, 3 * D_MODEL // (3 * WIDTH)),
                  tile(WIDTH, 0), tile(WIDTH, 0), tile(D_MODEL, 0),
                  const((8, WIDTH)), const((1, 3 * D_MODEL)),
                  const((3, WIDTH, D_MODEL)), const((D_MODEL, D_MODEL))],
        out_specs=tile(D_MODEL, 0),
        out_shape=jax.ShapeDtypeStruct((b, t, D_MODEL), F32),
        scratch_shapes=[pltpu.VMEM((8, WIDTH), F32)],
        compiler_params=pltpu.CompilerParams(
            dimension_semantics=("parallel", "arbitrary"), vmem_limit_bytes=VMEM_LIMIT),
        name="merge",
    )(zgc, zgc, yb, yc, x3d, conv_w, gate_b, w_branch, w_o)


def _ffn_kernel(x_ref, g_ref, wg_ref, wu_ref, cg_ref, cu_ref, wd_ref, fg_ref, o_ref,
                xn_ref, acc_ref, carry_g_ref, carry_u_ref, *, tm, final):
    f = pl.program_id(2)
    nf = pl.num_programs(2)

    @pl.when((pl.program_id(1) == 0) & (f == 0))
    def _():
        carry_g_ref[...] = jnp.zeros_like(carry_g_ref)
        carry_u_ref[...] = jnp.zeros_like(carry_u_ref)

    @pl.when(f == 0)
    def _():
        x = x_ref[0]
        ms = jnp.mean(x * x, axis=-1, keepdims=True)
        xn_ref[...] = (x * lax.rsqrt(ms + NORM_EPS) * g_ref[...]).astype(BF16)
        acc_ref[...] = jnp.zeros_like(acc_ref)

    xn = xn_ref[...]

    def conv(h, carry_ref, cw_ref):
        h1, h2 = _shift_rows(h, carry_ref.at[f], tm)
        return cw_ref[0:1, :] * h2 + cw_ref[1:2, :] * h1 + cw_ref[2:3, :] * h

    hg = conv(_dot(xn, wg_ref[...]), carry_g_ref, cg_ref)
    hu = conv(_dot(xn, wu_ref[...]), carry_u_ref, cu_ref)
    act = hg * _sigmoid(hg) * hu
    acc_ref[...] += _dot(act.astype(BF16), wd_ref[...])

    @pl.when(f == nf - 1)
    def _():
        y = x_ref[0] + acc_ref[...]
        if final:
            ms = jnp.mean(y * y, axis=-1, keepdims=True)
            y = y * lax.rsqrt(ms + NORM_EPS) * fg_ref[...]
        o_ref[0] = y


def _ffn(x3d, g, w_up, conv_w, w_down, final_g, final, tm=512, fc=1408):
    b, t, _ = x3d.shape
    nf = D_FF // fc
    return pl.pallas_call(
        functools.partial(_ffn_kernel, tm=tm, final=final),
        grid=(b, t // tm, nf),
        in_specs=[pl.BlockSpec((1, tm, D_MODEL), lambda i, j, f: (i, j, 0)),
                  pl.BlockSpec((1, D_MODEL), lambda i, j, f: (0, 0)),
                  pl.BlockSpec((D_MODEL, fc), lambda i, j, f: (0, f)),
                  pl.BlockSpec((D_MODEL, fc), lambda i, j, f: (0, f + nf)),
                  pl.BlockSpec((8, fc), lambda i, j, f: (0, f)),
                  pl.BlockSpec((8, fc), lambda i, j, f: (0, f + nf)),
                  pl.BlockSpec((fc, D_MODEL), lambda i, j, f: (f, 0)),
                  pl.BlockSpec((1, D_MODEL), lambda i, j, f: (0, 0))],
        out_specs=pl.BlockSpec((1, tm, D_MODEL), lambda i, j, f: (i, j, 0)),
        out_shape=jax.ShapeDtypeStruct((b, t, D_MODEL), F32),
        scratch_shapes=[pltpu.VMEM((tm, D_MODEL), BF16),
                        pltpu.VMEM((tm, D_MODEL), F32),
                        pltpu.VMEM((nf, 8, fc), F32),
                        pltpu.VMEM((nf, 8, fc), F32)],
        compiler_params=pltpu.CompilerParams(
            dimension_semantics=("parallel", "arbitrary", "arbitrary"),
            vmem_limit_bytes=VMEM_LIMIT),
        name="conv_ffn",
    )(x3d, g, w_up, w_up, conv_w, conv_w, w_down, final_g)


def _pad_rows(a, rows=8):
    return jnp.pad(a, ((0, rows - a.shape[0]), (0, 0)))


def _pack_w_in(w):
    conv_cols = 3 * WIDTH
    rwkv_cols = 3 * WIDTH + DECAY_RANK + ICLR_RANK + GATE_RANK
    n_heads = WIDTH // HEAD_DIM
    o1 = conv_cols
    o2 = o1 + rwkv_cols
    o3 = o2 + AT_COLS
    o4 = o3 + n_heads
    w_q = w[:, o2:o2 + WIDTH] * (HEAD_DIM ** -0.5 * LOG2E)
    w_ga = jnp.concatenate([w[:, o4:], w[:, :o1], w_q, w[:, o2 + WIDTH:o3]], axis=1)
    pad = jnp.zeros((w.shape[0], RW_COLS - RW_USED - n_heads), w.dtype)
    w_rw = jnp.concatenate([w[:, o1:o2], w[:, o3:o4], pad], axis=1)
    return w_ga.astype(BF16), w_rw.astype(BF16)


def kernel(x, norm1_g, w_in, gate_b, conv_mix_w, rwkv_mu, rwkv_w0, rwkv_w_up, rwkv_a0,
           rwkv_a_up, rwkv_g_up, rwkv_k_k, rwkv_k_a, rwkv_r_k, rwkv_gn_g, rwkv_gn_b,
           attn_forget_b, w_branch, w_o, norm2_g, ffn_w_up, ffn_conv_w, ffn_w_down,
           final_norm_g):
    b, t, d = x.shape
    depth = w_in.shape[0]
    n_heads = WIDTH // HEAD_DIM
    tf = 256
    tq = 512
    n_chunk = 4
    tt = n_chunk * CHUNK

    lane = np.arange(LANES)
    seg = jnp.asarray((lane[:, None] // HEAD_DIM) == (lane[None, :] // HEAD_DIM), BF16)
    pos = np.arange(tt)
    tri_c = jnp.asarray((pos[:, None] >= pos[None, :]) & (pos[:, None] // CHUNK == pos[None, :] // CHUNK),
                        BF16)
    prep_consts = _attn_prep_constants(tf)

    for l in range(depth):
        w_ga, w_rw = _pack_w_in(w_in[l])
        x2d = x.reshape(b * t, d)
        g1 = norm1_g[l][None, :]
        zga = _inproj(x2d, g1, w_ga, AT_COLS, BF16).reshape(b, t, GC_COLS + AT_COLS)
        zr = _inproj(x2d, g1, w_rw, RW_COLS // 2, F32).reshape(b, t, RW_COLS)

        wwa = jnp.zeros((LANES, 2 * WIDTH), F32)
        wwa = wwa.at[:DECAY_RANK, :WIDTH].set(rwkv_w_up[l]).at[DECAY_RANK:, WIDTH:].set(rwkv_a_up[l])
        w0a0 = jnp.concatenate([rwkv_w0[l], rwkv_a0[l]])[None, :]
        vecs = _pad_rows(jnp.stack([rwkv_k_k[l], rwkv_k_a[l], rwkv_r_k[l].reshape(WIDTH),
                                    rwkv_gn_g[l], rwkv_gn_b[l]]))
        y_b = _rwkv(zr, rwkv_mu[l][None, :], wwa.astype(BF16), w0a0,
                    rwkv_g_up[l].astype(BF16), vecs, seg, tri_c, n_chunk)

        b_f = jnp.pad(attn_forget_b[l], (0, LANES - n_heads))[None, :]
        q_aug, k_aug, v_aug = _attn_prep(zga, zr, b_f, prep_consts, tf)
        y_c = _attention(q_aug, k_aug, v_aug, tq)

        x = _merge(zga, y_b, y_c, x, _pad_rows(conv_mix_w[l]), gate_b[l][None, :],
                   w_branch[l].astype(BF16), w_o[l].astype(BF16))
        x = _ffn(x, norm2_g[l][None, :], ffn_w_up[l].astype(BF16), _pad_rows(ffn_conv_w[l]),
                 ffn_w_down[l].astype(BF16), final_norm_g[None, :], final=(l == depth - 1))
    return x
```
